```python
import jax, jax.numpy as jnp
from jax import lax
import numpy as np

D_MODEL = 2048
BATCH = 8
SEQ = 2048
DEPTH = 4

CHUNK = 64
N_MEM = 256
D_MIX = D_MODEL
DN_HEAD_DIM = 128
DN_HEADS = (D_MIX // 2) // DN_HEAD_DIM
DN_WIDTH = DN_HEADS * DN_HEAD_DIM
DN_CONV = 4
MLA_NOPE = 128
MLA_ROPE = 64
MLA_V = 128
MLA_HEADS = (D_MIX - DN_WIDTH) // MLA_V
MLA_Q_RANK = 512
MLA_KV_RANK = 256
ROPE_BASE = 10000.0
Q_BLOCK = 128
XA_HEADS = 4
XA_HEAD_DIM = D_MODEL // XA_HEADS
D_FF = ((8 * D_MODEL // 3 + 255) // 256) * 256
FFN_CONV = 3
EPS = 1e-6

kernel_name = "hybrid_deltanet_mla_memxattn_convffn"


def _in_split_sizes():
    return [DN_WIDTH, DN_WIDTH, DN_WIDTH, DN_WIDTH,
            DN_HEADS, DN_HEADS,
            MLA_Q_RANK,
            MLA_KV_RANK + MLA_ROPE]


def _in_cols():
    return sum(_in_split_sizes())


def _split_points():
    return [int(v) for v in np.cumsum(_in_split_sizes())[:-1]]


def rms_norm(x, gain):
    xf = x.astype(jnp.float32)
    y = xf * lax.rsqrt(jnp.mean(xf * xf, axis=-1, keepdims=True) + EPS)
    return (y * gain.astype(jnp.float32)).astype(x.dtype)


def l2_norm(x):
    xf = x.astype(jnp.float32)
    return xf * lax.rsqrt(jnp.sum(xf * xf, axis=-1, keepdims=True) + EPS)


def causal_dwconv(x, w):
    K, C = w.shape
    return lax.conv_general_dilated(
        x, w[:, None, :].astype(x.dtype), window_strides=(1,),
        padding=[(K - 1, 0)], dimension_numbers=("NWC", "WIO", "NWC"),
        feature_group_count=C)


def rope_cos_sin(positions):
    inv = ROPE_BASE ** (-jnp.arange(0, MLA_ROPE, 2, dtype=jnp.float32) / MLA_ROPE)
    ang = positions.astype(jnp.float32)[..., None] * inv
    return jnp.cos(ang), jnp.sin(ang)


def apply_rope(x, cos, sin):
    xf = x.astype(jnp.float32)
    x1, x2 = jnp.split(xf, 2, axis=-1)
    return jnp.concatenate([x1 * cos - x2 * sin, x2 * cos + x1 * sin], axis=-1).astype(x.dtype)


def chunk_gated_delta_rule(q, k, v, g, beta):
    B, S, H, Dk = q.shape
    Dv = v.shape[-1]
    N = S // CHUNK
    f32 = jnp.float32

    def chunks(t):
        return jnp.moveaxis(t.reshape(B, N, CHUNK, H, *t.shape[3:]), 3, 2)

    q = chunks(q.astype(f32)) * (Dk ** -0.5)
    k = chunks(k.astype(f32))
    v = chunks(v.astype(f32))
    beta = chunks(beta.astype(f32))
    G = jnp.cumsum(chunks(g.astype(f32)), axis=-1)

    incl = np.tril(np.ones((CHUNK, CHUNK), dtype=bool))
    strict = np.tril(np.ones((CHUNK, CHUNK), dtype=bool), -1)
    decay = jnp.exp(jnp.where(incl, G[..., :, None] - G[..., None, :], -jnp.inf))

    kb = k * beta[..., None]
    lower = jnp.where(strict, jnp.einsum("bnhik,bnhjk->bnhij", kb, k) * decay, 0.0)
    a_mat = lower + np.eye(CHUNK, dtype=np.float32)
    rhs = jnp.concatenate([v * beta[..., None], kb * jnp.exp(G)[..., None]], axis=-1)
    sol = lax.linalg.triangular_solve(a_mat, rhs, left_side=True, lower=True,
                                      unit_diagonal=True)
    u, w = sol[..., :Dv], sol[..., Dv:]

    attn = jnp.einsum("bnhik,bnhjk->bnhij", q, k) * decay
    q_dec = q * jnp.exp(G)[..., None]
    k_dec = k * jnp.exp(G[..., -1:] - G)[..., None]
    g_last = jnp.exp(G[..., -1])

    def step(state, xs):
        u_c, w_c, a_c, q_c, k_c, gl_c = xs
        v_new = u_c - jnp.einsum("bhck,bhkv->bhcv", w_c, state)
        o_c = (jnp.einsum("bhck,bhkv->bhcv", q_c, state)
               + jnp.einsum("bhij,bhjv->bhiv", a_c, v_new))
        state = state * gl_c[..., None, None] + jnp.einsum("bhck,bhcv->bhkv", k_c, v_new)
        return state, o_c

    xs = tuple(jnp.moveaxis(t, 1, 0) for t in (u, w, attn, q_dec, k_dec, g_last))
    state0 = jnp.zeros((B, H, Dk, Dv), f32)
    _, o = lax.scan(step, state0, xs)
    o = jnp.moveaxis(o, 0, 1)
    return jnp.moveaxis(o, 2, 3).reshape(B, S, H, Dv)


def gated_deltanet(q_raw, k_raw, v_raw, z, b, a, conv_w, a_log, dt_bias, out_norm):
    B, S, _ = q_raw.shape
    qkv = jax.nn.silu(causal_dwconv(jnp.concatenate([q_raw, k_raw, v_raw], axis=-1), conv_w))
    q, k, v = jnp.split(qkv, 3, axis=-1)
    q = l2_norm(q.reshape(B, S, DN_HEADS, DN_HEAD_DIM))
    k = l2_norm(k.reshape(B, S, DN_HEADS, DN_HEAD_DIM))
    v = v.reshape(B, S, DN_HEADS, DN_HEAD_DIM)
    beta = jax.nn.sigmoid(b.astype(jnp.float32))
    g = -jnp.exp(a_log.astype(jnp.float32)) * jax.nn.softplus(
        a.astype(jnp.float32) + dt_bias.astype(jnp.float32))
    o = chunk_gated_delta_rule(q, k, v, g, beta)
    zf = z.reshape(B, S, DN_HEADS, DN_HEAD_DIM).astype(jnp.float32)
    o = rms_norm(o, out_norm) * jax.nn.silu(zf)
    return o.reshape(B, S, DN_WIDTH).astype(q_raw.dtype)


def mla_attention(q_lat, kv_lat, q_norm, w_qb, kv_norm, w_kvb, cos, sin):
    B, S, _ = q_lat.shape
    q = (rms_norm(q_lat, q_norm) @ w_qb).reshape(B, S, MLA_HEADS, MLA_NOPE + MLA_ROPE)
    q_nope = q[..., :MLA_NOPE]
    q_pe = apply_rope(q[..., MLA_NOPE:], cos[:, :, None, :], sin[:, :, None, :])
    c_kv = kv_lat[..., :MLA_KV_RANK]
    k_pe = apply_rope(kv_lat[..., MLA_KV_RANK:], cos, sin)
    kv = (rms_norm(c_kv, kv_norm) @ w_kvb).reshape(B, S, MLA_HEADS, MLA_NOPE + MLA_V)
    k_nope, v = kv[..., :MLA_NOPE], kv[..., MLA_NOPE:]
    scale = (MLA_NOPE + MLA_ROPE) ** -0.5
    outs = []
    for blk in range(S // Q_BLOCK):
        q0 = blk * Q_BLOCK
        kend = q0 + Q_BLOCK
        s = (jnp.einsum("bqhd,bkhd->bhqk", q_nope[:, q0:kend], k_nope[:, :kend])
             + jnp.einsum("bqhr,bkr->bhqk", q_pe[:, q0:kend], k_pe[:, :kend]))
        s = s.astype(jnp.float32) * scale
        q_chunk = (q0 + np.arange(Q_BLOCK)) // CHUNK
        k_chunk = np.arange(kend) // CHUNK
        mask = k_chunk[None, :] <= q_chunk[:, None]
        p = jax.nn.softmax(jnp.where(mask, s, -jnp.inf), axis=-1).astype(v.dtype)
        outs.append(jnp.einsum("bhqk,bkhd->bqhd", p, v[:, :kend]))
    o = jnp.concatenate(outs, axis=1)
    return o.reshape(B, S, MLA_HEADS * MLA_V)


def memory_cross_attention(h, mem_n, wq, wk, wv, wo):
    B, S, _ = h.shape
    M = mem_n.shape[1]
    q = (h @ wq).reshape(B, S, XA_HEADS, XA_HEAD_DIM)
    k = (mem_n @ wk).reshape(B, M, XA_HEADS, XA_HEAD_DIM)
    v = (mem_n @ wv).reshape(B, M, XA_HEADS, XA_HEAD_DIM)
    s = jnp.einsum("bqhd,bmhd->bhqm", q, k).astype(jnp.float32) * (XA_HEAD_DIM ** -0.5)
    p = jax.nn.softmax(s, axis=-1).astype(v.dtype)
    o = jnp.einsum("bhqm,bmhd->bqhd", p, v).reshape(B, S, XA_HEADS * XA_HEAD_DIM)
    return o @ wo


def conv_ffn(h, w_up, conv_w, conv_b, w_down):
    u = causal_dwconv(h @ w_up, conv_w) + conv_b
    gate, up = jnp.split(u, 2, axis=-1)
    return (jax.nn.silu(gate) * up) @ w_down


def _fwd_setup_inputs(seed: int = 0) -> dict:
    key = jax.random.key(seed)
    ks = iter(jax.random.split(key, 40))
    f32 = jnp.float32

    def dense(shape, fan_in, scale=1.0):
        return jax.random.normal(next(ks), shape, f32) * (scale * fan_in ** -0.5)

    def gain(shape):
        return 1.0 + 0.01 * jax.random.normal(next(ks), shape, f32)

    x = jax.random.normal(next(ks), (BATCH, SEQ, D_MODEL), f32)
    mem = jax.random.normal(next(ks), (BATCH, N_MEM, D_MODEL), f32)
    offsets = jax.random.randint(next(ks), (BATCH, 1), 0, 64) * CHUNK
    positions = (offsets + jnp.arange(SEQ, dtype=jnp.int32)[None, :]).astype(jnp.int32)

    a_log = jnp.log(jax.random.uniform(next(ks), (DEPTH, DN_HEADS), f32, 1.0, 16.0))
    dt = jnp.exp(jax.random.uniform(next(ks), (DEPTH, DN_HEADS), f32,
                                    float(np.log(1e-3)), float(np.log(1e-1))))
    dt_bias = dt + jnp.log(-jnp.expm1(-dt))
    out_scale = 0.5
    return {
        "x": x,
        "mem": mem,
        "positions": positions,
        "norm_mix": gain((DEPTH, D_MODEL)),
        "w_in": dense((DEPTH, D_MODEL, _in_cols()), D_MODEL),
        "dn_conv": dense((DEPTH, DN_CONV, 3 * DN_WIDTH), DN_CONV),
        "dn_a_log": a_log,
        "dn_dt_bias": dt_bias,
        "dn_out_norm": gain((DEPTH, DN_HEAD_DIM)),
        "mla_q_norm": gain((DEPTH, MLA_Q_RANK)),
        "mla_w_qb": dense((DEPTH, MLA_Q_RANK, MLA_HEADS * (MLA_NOPE + MLA_ROPE)), MLA_Q_RANK),
        "mla_kv_norm": gain((DEPTH, MLA_KV_RANK)),
        "mla_w_kvb": dense((DEPTH, MLA_KV_RANK, MLA_HEADS * (MLA_NOPE + MLA_V)), MLA_KV_RANK),
        "w_out": dense((DEPTH, D_MIX, D_MODEL), D_MIX, out_scale),
        "mem_norm": gain((D_MODEL,)),
        "norm_xattn": gain((DEPTH, D_MODEL)),
        "xa_wq": dense((DEPTH, D_MODEL, XA_HEADS * XA_HEAD_DIM), D_MODEL),
        "xa_wk": dense((DEPTH, D_MODEL, XA_HEADS * XA_HEAD_DIM), D_MODEL),
        "xa_wv": dense((DEPTH, D_MODEL, XA_HEADS * XA_HEAD_DIM), D_MODEL),
        "xa_wo": dense((DEPTH, XA_HEADS * XA_HEAD_DIM, D_MODEL), D_MODEL, out_scale),
        "norm_ffn": gain((DEPTH, D_MODEL)),
        "ffn_w_up": dense((DEPTH, D_MODEL, 2 * D_FF), D_MODEL),
        "ffn_conv": dense((DEPTH, FFN_CONV, 2 * D_FF), FFN_CONV),
        "ffn_conv_bias": 0.01 * jax.random.normal(next(ks), (DEPTH, 2 * D_FF), f32),
        "ffn_w_down": dense((DEPTH, D_FF, D_MODEL), D_FF, out_scale),
        "norm_final": gain((D_MODEL,)),
    }


def _fwd_reference(x, mem, positions, norm_mix, w_in, dn_conv, dn_a_log, dn_dt_bias,
              dn_out_norm, mla_q_norm, mla_w_qb, mla_kv_norm, mla_w_kvb, w_out,
              mem_norm, norm_xattn, xa_wq, xa_wk, xa_wv, xa_wo, norm_ffn,
              ffn_w_up, ffn_conv, ffn_conv_bias, ffn_w_down, norm_final):
    cos, sin = rope_cos_sin(positions)
    mem_n = rms_norm(mem, mem_norm)
    split_points = _split_points()
    h = x
    for l in range(DEPTH):
        u = rms_norm(h, norm_mix[l])
        proj = u @ w_in[l]
        dq, dk, dv, dz, db, da, mq, mkv = jnp.split(proj, split_points, axis=-1)
        o_dn = gated_deltanet(dq, dk, dv, dz, db, da, dn_conv[l], dn_a_log[l],
                              dn_dt_bias[l], dn_out_norm[l])
        o_mla = mla_attention(mq, mkv, mla_q_norm[l], mla_w_qb[l], mla_kv_norm[l],
                              mla_w_kvb[l], cos, sin)
        h = h + jnp.concatenate([o_dn.astype(h.dtype), o_mla.astype(h.dtype)], axis=-1) @ w_out[l]
        h = h + memory_cross_attention(rms_norm(h, norm_xattn[l]), mem_n, xa_wq[l],
                                       xa_wk[l], xa_wv[l], xa_wo[l])
        h = h + conv_ffn(rms_norm(h, norm_ffn[l]), ffn_w_up[l], ffn_conv[l],
                         ffn_conv_bias[l], ffn_w_down[l])
    return rms_norm(h, norm_final)


import jax as _jax
import jax.numpy as _jnp

TWIN_FORMAT = 'train_step'
FWD_PARAMS = ['x', 'mem', 'positions', 'norm_mix', 'w_in', 'dn_conv', 'dn_a_log', 'dn_dt_bias', 'dn_out_norm', 'mla_q_norm', 'mla_w_qb', 'mla_kv_norm', 'mla_w_kvb', 'w_out', 'mem_norm', 'norm_xattn', 'xa_wq', 'xa_wk', 'xa_wv', 'xa_wo', 'norm_ffn', 'ffn_w_up', 'ffn_conv', 'ffn_conv_bias', 'ffn_w_down', 'norm_final']
TWIN_WEIGHTS = ['norm_mix', 'w_in', 'dn_conv', 'dn_a_log', 'dn_dt_bias', 'dn_out_norm', 'mla_q_norm', 'mla_w_qb', 'mla_kv_norm', 'mla_w_kvb', 'w_out', 'mem_norm', 'norm_xattn', 'xa_wq', 'xa_wk', 'xa_wv', 'xa_wo', 'norm_ffn', 'ffn_w_up', 'ffn_conv', 'ffn_conv_bias', 'ffn_w_down', 'norm_final']
TWIN_DIFF_INPUT = 'x'
TWIN_INPUTS = ['x', 'mem', 'positions', 'norm_mix', 'w_in', 'dn_conv', 'dn_a_log', 'dn_dt_bias', 'dn_out_norm', 'mla_q_norm', 'mla_w_qb', 'mla_kv_norm', 'mla_w_kvb', 'w_out', 'mem_norm', 'norm_xattn', 'xa_wq', 'xa_wk', 'xa_wv', 'xa_wo', 'norm_ffn', 'ffn_w_up', 'ffn_conv', 'ffn_conv_bias', 'ffn_w_down', 'norm_final', 'loss_target', 'm_norm_mix', 'm_w_in', 'm_dn_conv', 'm_dn_a_log', 'm_dn_dt_bias', 'm_dn_out_norm', 'm_mla_q_norm', 'm_mla_w_qb', 'm_mla_kv_norm', 'm_mla_w_kvb', 'm_w_out', 'm_mem_norm', 'm_norm_xattn', 'm_xa_wq', 'm_xa_wk', 'm_xa_wv', 'm_xa_wo', 'm_norm_ffn', 'm_ffn_w_up', 'm_ffn_conv', 'm_ffn_conv_bias', 'm_ffn_w_down', 'm_norm_final', 'v_norm_mix', 'v_w_in', 'v_dn_conv', 'v_dn_a_log', 'v_dn_dt_bias', 'v_dn_out_norm', 'v_mla_q_norm', 'v_mla_w_qb', 'v_mla_kv_norm', 'v_mla_w_kvb', 'v_w_out', 'v_mem_norm', 'v_norm_xattn', 'v_xa_wq', 'v_xa_wk', 'v_xa_wv', 'v_xa_wo', 'v_norm_ffn', 'v_ffn_w_up', 'v_ffn_conv', 'v_ffn_conv_bias', 'v_ffn_w_down', 'v_norm_final']
TWIN_OUTPUTS = ['loss', 'grad_x', 'grad_norm_mix', 'grad_w_in', 'grad_dn_conv', 'grad_dn_a_log', 'grad_dn_dt_bias', 'grad_dn_out_norm', 'grad_mla_q_norm', 'grad_mla_w_qb', 'grad_mla_kv_norm', 'grad_mla_w_kvb', 'grad_w_out', 'grad_mem_norm', 'grad_norm_xattn', 'grad_xa_wq', 'grad_xa_wk', 'grad_xa_wv', 'grad_xa_wo', 'grad_norm_ffn', 'grad_ffn_w_up', 'grad_ffn_conv', 'grad_ffn_conv_bias', 'grad_ffn_w_down', 'grad_norm_final', 'delta_norm_mix', 'delta_w_in', 'delta_dn_conv', 'delta_dn_a_log', 'delta_dn_dt_bias', 'delta_dn_out_norm', 'delta_mla_q_norm', 'delta_mla_w_qb', 'delta_mla_kv_norm', 'delta_mla_w_kvb', 'delta_w_out', 'delta_mem_norm', 'delta_norm_xattn', 'delta_xa_wq', 'delta_xa_wk', 'delta_xa_wv', 'delta_xa_wo', 'delta_norm_ffn', 'delta_ffn_w_up', 'delta_ffn_conv', 'delta_ffn_conv_bias', 'delta_ffn_w_down', 'delta_norm_final', 'new_m_norm_mix', 'new_m_w_in', 'new_m_dn_conv', 'new_m_dn_a_log', 'new_m_dn_dt_bias', 'new_m_dn_out_norm', 'new_m_mla_q_norm', 'new_m_mla_w_qb', 'new_m_mla_kv_norm', 'new_m_mla_w_kvb', 'new_m_w_out', 'new_m_mem_norm', 'new_m_norm_xattn', 'new_m_xa_wq', 'new_m_xa_wk', 'new_m_xa_wv', 'new_m_xa_wo', 'new_m_norm_ffn', 'new_m_ffn_w_up', 'new_m_ffn_conv', 'new_m_ffn_conv_bias', 'new_m_ffn_w_down', 'new_m_norm_final', 'new_v_norm_mix', 'new_v_w_in', 'new_v_dn_conv', 'new_v_dn_a_log', 'new_v_dn_dt_bias', 'new_v_dn_out_norm', 'new_v_mla_q_norm', 'new_v_mla_w_qb', 'new_v_mla_kv_norm', 'new_v_mla_w_kvb', 'new_v_w_out', 'new_v_mem_norm', 'new_v_norm_xattn', 'new_v_xa_wq', 'new_v_xa_wk', 'new_v_xa_wv', 'new_v_xa_wo', 'new_v_norm_ffn', 'new_v_ffn_w_up', 'new_v_ffn_conv', 'new_v_ffn_conv_bias', 'new_v_ffn_w_down', 'new_v_norm_final']
TWIN_LEAF_KINDS = {'loss': 'loss', 'grad_x': 'grad_x', 'grad_norm_mix': 'grad_w', 'grad_w_in': 'grad_w', 'grad_dn_conv': 'grad_w', 'grad_dn_a_log': 'grad_w', 'grad_dn_dt_bias': 'grad_w', 'grad_dn_out_norm': 'grad_w', 'grad_mla_q_norm': 'grad_w', 'grad_mla_w_qb': 'grad_w', 'grad_mla_kv_norm': 'grad_w', 'grad_mla_w_kvb': 'grad_w', 'grad_w_out': 'grad_w', 'grad_mem_norm': 'grad_w', 'grad_norm_xattn': 'grad_w', 'grad_xa_wq': 'grad_w', 'grad_xa_wk': 'grad_w', 'grad_xa_wv': 'grad_w', 'grad_xa_wo': 'grad_w', 'grad_norm_ffn': 'grad_w', 'grad_ffn_w_up': 'grad_w', 'grad_ffn_conv': 'grad_w', 'grad_ffn_conv_bias': 'grad_w', 'grad_ffn_w_down': 'grad_w', 'grad_norm_final': 'grad_w', 'delta_norm_mix': 'delta_w', 'delta_w_in': 'delta_w', 'delta_dn_conv': 'delta_w', 'delta_dn_a_log': 'delta_w', 'delta_dn_dt_bias': 'delta_w', 'delta_dn_out_norm': 'delta_w', 'delta_mla_q_norm': 'delta_w', 'delta_mla_w_qb': 'delta_w', 'delta_mla_kv_norm': 'delta_w', 'delta_mla_w_kvb': 'delta_w', 'delta_w_out': 'delta_w', 'delta_mem_norm': 'delta_w', 'delta_norm_xattn': 'delta_w', 'delta_xa_wq': 'delta_w', 'delta_xa_wk': 'delta_w', 'delta_xa_wv': 'delta_w', 'delta_xa_wo': 'delta_w', 'delta_norm_ffn': 'delta_w', 'delta_ffn_w_up': 'delta_w', 'delta_ffn_conv': 'delta_w', 'delta_ffn_conv_bias': 'delta_w', 'delta_ffn_w_down': 'delta_w', 'delta_norm_final': 'delta_w', 'new_m_norm_mix': 'new_m', 'new_m_w_in': 'new_m', 'new_m_dn_conv': 'new_m', 'new_m_dn_a_log': 'new_m', 'new_m_dn_dt_bias': 'new_m', 'new_m_dn_out_norm': 'new_m', 'new_m_mla_q_norm': 'new_m', 'new_m_mla_w_qb': 'new_m', 'new_m_mla_kv_norm': 'new_m', 'new_m_mla_w_kvb': 'new_m', 'new_m_w_out': 'new_m', 'new_m_mem_norm': 'new_m', 'new_m_norm_xattn': 'new_m', 'new_m_xa_wq': 'new_m', 'new_m_xa_wk': 'new_m', 'new_m_xa_wv': 'new_m', 'new_m_xa_wo': 'new_m', 'new_m_norm_ffn': 'new_m', 'new_m_ffn_w_up': 'new_m', 'new_m_ffn_conv': 'new_m', 'new_m_ffn_conv_bias': 'new_m', 'new_m_ffn_w_down': 'new_m', 'new_m_norm_final': 'new_m', 'new_v_norm_mix': 'new_v', 'new_v_w_in': 'new_v', 'new_v_dn_conv': 'new_v', 'new_v_dn_a_log': 'new_v', 'new_v_dn_dt_bias': 'new_v', 'new_v_dn_out_norm': 'new_v', 'new_v_mla_q_norm': 'new_v', 'new_v_mla_w_qb': 'new_v', 'new_v_mla_kv_norm': 'new_v', 'new_v_mla_w_kvb': 'new_v', 'new_v_w_out': 'new_v', 'new_v_mem_norm': 'new_v', 'new_v_norm_xattn': 'new_v', 'new_v_xa_wq': 'new_v', 'new_v_xa_wk': 'new_v', 'new_v_xa_wv': 'new_v', 'new_v_xa_wo': 'new_v', 'new_v_norm_ffn': 'new_v', 'new_v_ffn_w_up': 'new_v', 'new_v_ffn_conv': 'new_v', 'new_v_ffn_conv_bias': 'new_v', 'new_v_ffn_w_down': 'new_v', 'new_v_norm_final': 'new_v'}


def _forward(args):
    return _fwd_reference(*[args[k] for k in FWD_PARAMS])


def _output_shape():
    out = _jax.eval_shape(lambda: _forward(_fwd_setup_inputs(0)))
    return out.shape, out.dtype

N_MICROBATCH = 1
ADAM_LR = 0.001
ADAM_B1 = 0.9
ADAM_B2 = 0.999
ADAM_EPS = 1e-08
ADAM_WD = 0.01
ADAM_STEP = 10
PER_EXAMPLE_BATCH_AXIS = {'x': 0, 'mem': 0, 'positions': 0, 'loss_target': 0}
SHARED_INPUTS = []
_WEIGHT_DTYPES = {'norm_mix': _jnp.float32, 'w_in': _jnp.float32, 'dn_conv': _jnp.float32, 'dn_a_log': _jnp.float32, 'dn_dt_bias': _jnp.float32, 'dn_out_norm': _jnp.float32, 'mla_q_norm': _jnp.float32, 'mla_w_qb': _jnp.float32, 'mla_kv_norm': _jnp.float32, 'mla_w_kvb': _jnp.float32, 'w_out': _jnp.float32, 'mem_norm': _jnp.float32, 'norm_xattn': _jnp.float32, 'xa_wq': _jnp.float32, 'xa_wk': _jnp.float32, 'xa_wv': _jnp.float32, 'xa_wo': _jnp.float32, 'norm_ffn': _jnp.float32, 'ffn_w_up': _jnp.float32, 'ffn_conv': _jnp.float32, 'ffn_conv_bias': _jnp.float32, 'ffn_w_down': _jnp.float32, 'norm_final': _jnp.float32}
MOMENT_SCALE = {'norm_mix': 2.335187e-02, 'w_in': 1.507084e-02, 'dn_conv': 1.488800e-02, 'dn_a_log': 9.161842e-02, 'dn_dt_bias': 8.830010e-02, 'dn_out_norm': 5.356985e-02, 'mla_q_norm': 6.644281e-03, 'mla_w_qb': 3.880796e-03, 'mla_kv_norm': 1.508003e-02, 'mla_w_kvb': 4.821258e-03, 'w_out': 2.787583e-02, 'mem_norm': 9.888597e-03, 'norm_xattn': 3.324135e-03, 'xa_wq': 3.308198e-03, 'xa_wk': 3.307126e-03, 'xa_wv': 3.478493e-03, 'xa_wo': 6.957884e-03, 'norm_ffn': 2.632035e-02, 'ffn_w_up': 1.132741e-02, 'ffn_conv': 1.132152e-02, 'ffn_conv_bias': 1.120409e-02, 'ffn_w_down': 3.699500e-02, 'norm_final': 7.993137e+00}


def _to_microbatches(a, axis):
    t = _jnp.moveaxis(a, axis, 0)
    t = t.reshape((N_MICROBATCH, t.shape[0] // N_MICROBATCH) + t.shape[1:])
    return _jnp.moveaxis(t, 1, axis + 1)


def setup_inputs(seed: int = 0) -> dict:
    inp = _fwd_setup_inputs(seed)
    key = _jax.random.fold_in(_jax.random.key(seed), 7919)
    shape, _ = _output_shape()
    out = dict(inp)
    out["loss_target"] = _jax.random.normal(_jax.random.fold_in(key, 0), shape, _jnp.float32)
    for i, name in enumerate(TWIN_WEIGHTS):
        w = inp[name].astype(_jnp.float32)
        if MOMENT_SCALE is None:
            s = _jnp.sqrt(_jnp.mean(_jnp.square(w)) + 1e-30)
        else:
            s = MOMENT_SCALE[name]
        km, kv = _jax.random.split(_jax.random.fold_in(key, i + 1))
        out[name] = w
        out["m_" + name] = s * _jax.random.normal(km, w.shape, _jnp.float32)
        out["v_" + name] = (s * s) * _jax.random.uniform(kv, w.shape, _jnp.float32, 0.5, 1.5)
    if N_MICROBATCH > 1:
        for name, axis in PER_EXAMPLE_BATCH_AXIS.items():
            out[name] = _to_microbatches(out[name], axis)
    return {'x': out['x'], 'mem': out['mem'], 'positions': out['positions'], 'norm_mix': out['norm_mix'], 'w_in': out['w_in'], 'dn_conv': out['dn_conv'], 'dn_a_log': out['dn_a_log'], 'dn_dt_bias': out['dn_dt_bias'], 'dn_out_norm': out['dn_out_norm'], 'mla_q_norm': out['mla_q_norm'], 'mla_w_qb': out['mla_w_qb'], 'mla_kv_norm': out['mla_kv_norm'], 'mla_w_kvb': out['mla_w_kvb'], 'w_out': out['w_out'], 'mem_norm': out['mem_norm'], 'norm_xattn': out['norm_xattn'], 'xa_wq': out['xa_wq'], 'xa_wk': out['xa_wk'], 'xa_wv': out['xa_wv'], 'xa_wo': out['xa_wo'], 'norm_ffn': out['norm_ffn'], 'ffn_w_up': out['ffn_w_up'], 'ffn_conv': out['ffn_conv'], 'ffn_conv_bias': out['ffn_conv_bias'], 'ffn_w_down': out['ffn_w_down'], 'norm_final': out['norm_final'], 'loss_target': out['loss_target'], 'm_norm_mix': out['m_norm_mix'], 'm_w_in': out['m_w_in'], 'm_dn_conv': out['m_dn_conv'], 'm_dn_a_log': out['m_dn_a_log'], 'm_dn_dt_bias': out['m_dn_dt_bias'], 'm_dn_out_norm': out['m_dn_out_norm'], 'm_mla_q_norm': out['m_mla_q_norm'], 'm_mla_w_qb': out['m_mla_w_qb'], 'm_mla_kv_norm': out['m_mla_kv_norm'], 'm_mla_w_kvb': out['m_mla_w_kvb'], 'm_w_out': out['m_w_out'], 'm_mem_norm': out['m_mem_norm'], 'm_norm_xattn': out['m_norm_xattn'], 'm_xa_wq': out['m_xa_wq'], 'm_xa_wk': out['m_xa_wk'], 'm_xa_wv': out['m_xa_wv'], 'm_xa_wo': out['m_xa_wo'], 'm_norm_ffn': out['m_norm_ffn'], 'm_ffn_w_up': out['m_ffn_w_up'], 'm_ffn_conv': out['m_ffn_conv'], 'm_ffn_conv_bias': out['m_ffn_conv_bias'], 'm_ffn_w_down': out['m_ffn_w_down'], 'm_norm_final': out['m_norm_final'], 'v_norm_mix': out['v_norm_mix'], 'v_w_in': out['v_w_in'], 'v_dn_conv': out['v_dn_conv'], 'v_dn_a_log': out['v_dn_a_log'], 'v_dn_dt_bias': out['v_dn_dt_bias'], 'v_dn_out_norm': out['v_dn_out_norm'], 'v_mla_q_norm': out['v_mla_q_norm'], 'v_mla_w_qb': out['v_mla_w_qb'], 'v_mla_kv_norm': out['v_mla_kv_norm'], 'v_mla_w_kvb': out['v_mla_w_kvb'], 'v_w_out': out['v_w_out'], 'v_mem_norm': out['v_mem_norm'], 'v_norm_xattn': out['v_norm_xattn'], 'v_xa_wq': out['v_xa_wq'], 'v_xa_wk': out['v_xa_wk'], 'v_xa_wv': out['v_xa_wv'], 'v_xa_wo': out['v_xa_wo'], 'v_norm_ffn': out['v_norm_ffn'], 'v_ffn_w_up': out['v_ffn_w_up'], 'v_ffn_conv': out['v_ffn_conv'], 'v_ffn_conv_bias': out['v_ffn_conv_bias'], 'v_ffn_w_down': out['v_ffn_w_down'], 'v_norm_final': out['v_norm_final']}


def _loss(weights, diff, rest, loss_target):
    with _jax.named_scope("forward"):
        args = {**rest, TWIN_DIFF_INPUT: diff, **{k: w.astype(_WEIGHT_DTYPES[k]) for k, w in weights.items()}}
        y = _forward(args)
    with _jax.named_scope("loss_head"):
        err = _jnp.square(y.astype(_jnp.float32) - loss_target)
        return 0.5 * _jnp.sum(_jnp.mean(err, axis=-1)) if err.ndim else 0.5 * err


def _adamw(w, g, m, v):
    m = ADAM_B1 * m + (1.0 - ADAM_B1) * g
    v = ADAM_B2 * v + (1.0 - ADAM_B2) * _jnp.square(g)
    m_hat = m / (1.0 - ADAM_B1 ** ADAM_STEP)
    v_hat = v / (1.0 - ADAM_B2 ** ADAM_STEP)
    delta = -ADAM_LR * (m_hat / (_jnp.sqrt(v_hat) + ADAM_EPS) + ADAM_WD * w)
    return delta, m, v


def reference(x, mem, positions, norm_mix, w_in, dn_conv, dn_a_log, dn_dt_bias, dn_out_norm, mla_q_norm, mla_w_qb, mla_kv_norm, mla_w_kvb, w_out, mem_norm, norm_xattn, xa_wq, xa_wk, xa_wv, xa_wo, norm_ffn, ffn_w_up, ffn_conv, ffn_conv_bias, ffn_w_down, norm_final, loss_target, m_norm_mix, m_w_in, m_dn_conv, m_dn_a_log, m_dn_dt_bias, m_dn_out_norm, m_mla_q_norm, m_mla_w_qb, m_mla_kv_norm, m_mla_w_kvb, m_w_out, m_mem_norm, m_norm_xattn, m_xa_wq, m_xa_wk, m_xa_wv, m_xa_wo, m_norm_ffn, m_ffn_w_up, m_ffn_conv, m_ffn_conv_bias, m_ffn_w_down, m_norm_final, v_norm_mix, v_w_in, v_dn_conv, v_dn_a_log, v_dn_dt_bias, v_dn_out_norm, v_mla_q_norm, v_mla_w_qb, v_mla_kv_norm, v_mla_w_kvb, v_w_out, v_mem_norm, v_norm_xattn, v_xa_wq, v_xa_wk, v_xa_wv, v_xa_wo, v_norm_ffn, v_ffn_w_up, v_ffn_conv, v_ffn_conv_bias, v_ffn_w_down, v_norm_final):
    given = dict(x=x, mem=mem, positions=positions, norm_mix=norm_mix, w_in=w_in, dn_conv=dn_conv, dn_a_log=dn_a_log, dn_dt_bias=dn_dt_bias, dn_out_norm=dn_out_norm, mla_q_norm=mla_q_norm, mla_w_qb=mla_w_qb, mla_kv_norm=mla_kv_norm, mla_w_kvb=mla_w_kvb, w_out=w_out, mem_norm=mem_norm, norm_xattn=norm_xattn, xa_wq=xa_wq, xa_wk=xa_wk, xa_wv=xa_wv, xa_wo=xa_wo, norm_ffn=norm_ffn, ffn_w_up=ffn_w_up, ffn_conv=ffn_conv, ffn_conv_bias=ffn_conv_bias, ffn_w_down=ffn_w_down, norm_final=norm_final, loss_target=loss_target, m_norm_mix=m_norm_mix, m_w_in=m_w_in, m_dn_conv=m_dn_conv, m_dn_a_log=m_dn_a_log, m_dn_dt_bias=m_dn_dt_bias, m_dn_out_norm=m_dn_out_norm, m_mla_q_norm=m_mla_q_norm, m_mla_w_qb=m_mla_w_qb, m_mla_kv_norm=m_mla_kv_norm, m_mla_w_kvb=m_mla_w_kvb, m_w_out=m_w_out, m_mem_norm=m_mem_norm, m_norm_xattn=m_norm_xattn, m_xa_wq=m_xa_wq, m_xa_wk=m_xa_wk, m_xa_wv=m_xa_wv, m_xa_wo=m_xa_wo, m_norm_ffn=m_norm_ffn, m_ffn_w_up=m_ffn_w_up, m_ffn_conv=m_ffn_conv, m_ffn_conv_bias=m_ffn_conv_bias, m_ffn_w_down=m_ffn_w_down, m_norm_final=m_norm_final, v_norm_mix=v_norm_mix, v_w_in=v_w_in, v_dn_conv=v_dn_conv, v_dn_a_log=v_dn_a_log, v_dn_dt_bias=v_dn_dt_bias, v_dn_out_norm=v_dn_out_norm, v_mla_q_norm=v_mla_q_norm, v_mla_w_qb=v_mla_w_qb, v_mla_kv_norm=v_mla_kv_norm, v_mla_w_kvb=v_mla_w_kvb, v_w_out=v_w_out, v_mem_norm=v_mem_norm, v_norm_xattn=v_norm_xattn, v_xa_wq=v_xa_wq, v_xa_wk=v_xa_wk, v_xa_wv=v_xa_wv, v_xa_wo=v_xa_wo, v_norm_ffn=v_norm_ffn, v_ffn_w_up=v_ffn_w_up, v_ffn_conv=v_ffn_conv, v_ffn_conv_bias=v_ffn_conv_bias, v_ffn_w_down=v_ffn_w_down, v_norm_final=v_norm_final)
    weights = {n: given[n] for n in TWIN_WEIGHTS}
    shared = {n: given[n] for n in SHARED_INPUTS}
    per_example = {n: given[n] for n in ['x', 'mem', 'positions']}
    grad_fn = _jax.value_and_grad(_loss, argnums=(0, 1))

    def one_microbatch(ex, loss_target):
        ex = dict(ex)
        diff = ex.pop(TWIN_DIFF_INPUT)
        return grad_fn(weights, diff, {**shared, **ex}, loss_target)

    if N_MICROBATCH == 1:
        loss, (grad_w, grad_x) = one_microbatch(per_example, given["loss_target"])
    else:
        def body(carry, xs):
            loss_sum, grad_sum = carry
            l_k, (gw_k, gx_k) = one_microbatch(xs[0], xs[1])
            with _jax.named_scope("update"):
                return (loss_sum + l_k, _jax.tree.map(_jnp.add, grad_sum, gw_k)), gx_k

        init = (_jnp.zeros((), _jnp.float32), _jax.tree.map(_jnp.zeros_like, weights))
        (loss, grad_w), grad_x = _jax.lax.scan(body, init, (per_example, given["loss_target"]))
    with _jax.named_scope("update"):
        delta_w, new_m, new_v = {}, {}, {}
        for n in TWIN_WEIGHTS:
            delta_w[n], new_m[n], new_v[n] = _adamw(weights[n], grad_w[n], given["m_" + n], given["v_" + n])
    return (loss, grad_x, *[grad_w[n] for n in TWIN_WEIGHTS], *[delta_w[n] for n in TWIN_WEIGHTS],
            *[new_m[n] for n in TWIN_WEIGHTS], *[new_v[n] for n in TWIN_WEIGHTS])
```

```python
import functools

import numpy as np
import jax
import jax.numpy as jnp
from jax import lax
from jax.experimental import pallas as pl
from jax.experimental.pallas import tpu as pltpu

F32 = jnp.float32
BF16 = jnp.bfloat16
MESH = pl.DeviceIdType.MESH
HIGHEST = lax.Precision.HIGHEST

LANES = 128
VMEM_LIMIT = 56 * 1024 * 1024
N_CHIPS = 4
N_DEV = 8

CHUNK = 64
DN_HEAD_DIM = 128
MLA_NOPE, MLA_ROPE, MLA_V = 128, 64, 128
MLA_Q_RANK, MLA_KV_RANK = 512, 256
XA_HEADS = 4
ROPE_BASE = 10000.0
EPS = 1e-6

ADAM_LR, ADAM_B1, ADAM_B2, ADAM_EPS, ADAM_WD, ADAM_STEP = 0.001, 0.9, 0.999, 1e-08, 0.01, 10

NN = (((1,), (0,)), ((), ()))
NT = (((1,), (1,)), ((), ()))
TN = (((0,), (0,)), ((), ()))


def _pick(dim, cands):
    for c in cands:
        if c <= dim and dim % c == 0:
            return c
    return dim


def _params(sem=None):
    return pltpu.CompilerParams(dimension_semantics=sem, vmem_limit_bytes=VMEM_LIMIT)


def _mm_call(name, a, b, res, dims, grid, a_spec, b_spec, o_spec, out_shape, acc_shape):
    nk = grid[3]
    has_res = res is not None

    def body(*refs):
        if has_res:
            a_ref, b_ref, r_ref, o_ref, acc = refs
        else:
            a_ref, b_ref, o_ref, acc = refs
        kk = pl.program_id(3)

        @pl.when(kk == 0)
        def _():
            acc[...] = jnp.zeros_like(acc)

        acc[...] += lax.dot_general(a_ref[...].astype(BF16), b_ref[...].astype(BF16), dims,
                                    preferred_element_type=F32)

        @pl.when(kk == nk - 1)
        def _():
            r = acc[...]
            if has_res:
                r = r + r_ref[...]
            o_ref[...] = r.astype(o_ref.dtype)

    in_specs = [a_spec, b_spec] + ([o_spec] if has_res else [])
    args = (a, b) + ((res,) if has_res else ())
    return pl.pallas_call(
        body, out_shape=out_shape, grid=grid, in_specs=in_specs, out_specs=o_spec,
        scratch_shapes=[pltpu.VMEM(acc_shape, F32)], name=name,
        compiler_params=_params(("parallel", "parallel", "parallel", "arbitrary")),
    )(*args)


_TM = (1024, 512, 256, 128, 64, 32, 16, 8)
_TN = (1024, 768, 640, 512, 384, 256, 128)


def _ident(nb):
    return nb


def mm_nn_raw(a, w, res=None, tn=None, colperm=_ident, name="mm_nn"):
    M, K = a.shape
    G, _, C = w.shape
    tm = _pick(M, _TM)
    tn = tn or _pick(C, _TN)
    tk = K if K <= 2048 else _pick(K, (1408, 1024, 512, 256, 128))
    nj = C // tn
    grid = (G, M // tm, nj, K // tk)
    return _mm_call(
        name, a, w, res, NN, grid,
        pl.BlockSpec((tm, tk), lambda g, i, j, k: (i, k)),
        pl.BlockSpec((None, tk, tn), lambda g, i, j, k: (g, k, j)),
        pl.BlockSpec((tm, tn), lambda g, i, j, k: (i, colperm(g * nj + j))),
        jax.ShapeDtypeStruct((M, G * C), F32), (tm, tn))


def mm_nt_raw(d, w, tn=None, colperm=_ident, name="mm_nt"):
    M = d.shape[0]
    G, K, C = w.shape
    tm = _pick(M, _TM)
    tko = _pick(K, _TN)
    tc = tn or _pick(C, (512, 384, 256, 128))
    ncb = C // tc
    grid = (1, M // tm, K // tko, G * ncb)
    return _mm_call(
        name, d, w, None, NT, grid,
        pl.BlockSpec((tm, tc), lambda g, i, j, k: (i, colperm(k))),
        pl.BlockSpec((None, tko, tc), lambda g, i, j, k: (k // ncb, j, k % ncb)),
        pl.BlockSpec((tm, tko), lambda g, i, j, k: (i, j)),
        jax.ShapeDtypeStruct((M, K), F32), (tm, tko))


def mm_tn_raw(a, d, G, tn=None, colperm=_ident, name="mm_tn"):
    M, K = a.shape
    C = d.shape[1] // G
    tko = _pick(K, _TM)
    tn = tn or _pick(C, _TN)
    tm = _pick(M, (512, 256, 128, 64, 32, 16, 8))
    nj = C // tn
    grid = (G, K // tko, nj, M // tm)
    return _mm_call(
        name, a, d, None, TN, grid,
        pl.BlockSpec((tm, tko), lambda g, i, j, k: (k, i)),
        pl.BlockSpec((tm, tn), lambda g, i, j, k: (k, colperm(g * nj + j))),
        pl.BlockSpec((None, tko, tn), lambda g, i, j, k: (g, i, j)),
        jax.ShapeDtypeStruct((G, K, C), BF16), (tko, tn))


def _make_linear(tn, colperm, with_res):
    @jax.custom_vjp
    def lin(a, w, res):
        return mm_nn_raw(a.astype(BF16), w, res if with_res else None, tn=tn, colperm=colperm)

    def fwd(a, w, res):
        a16 = a.astype(BF16)
        return mm_nn_raw(a16, w, res if with_res else None, tn=tn, colperm=colperm), (a16, w)

    def bwd(saved, dout):
        a16, w = saved
        d16 = dout.astype(BF16)
        da = mm_nt_raw(d16, w, tn=tn, colperm=colperm)
        dw = mm_tn_raw(a16, d16, w.shape[0], tn=tn, colperm=colperm)
        return da, dw, (dout if with_res else None)

    lin.defvjp(fwd, bwd)
    return lin


def linear(a, w, res=None, tn=None, colperm=_ident):
    if res is None:
        return _make_linear(tn, colperm, False)(a, w, None)
    return _make_linear(tn, colperm, True)(a, w, res)


def _row_tile(R, D):
    return _pick(R, tuple(t for t in (2048, 1024, 512, 256, 128, 64, 32, 16, 8) if t * D * 4 <= (2 << 20)))


def _norm_fwd_call(x, gain, eps, inv_n):
    R, D = x.shape
    tr = _row_tile(R, D)
    has_gain = gain is not None

    def body(*refs):
        if has_gain:
            x_ref, g_ref, y_ref = refs
        else:
            x_ref, y_ref = refs
        xv = x_ref[...]
        r = lax.rsqrt(jnp.sum(xv * xv, axis=-1, keepdims=True) * inv_n + eps)
        y = xv * r
        if has_gain:
            y = y * g_ref[...]
        y_ref[...] = y

    row = pl.BlockSpec((tr, D), lambda i: (i, 0))
    in_specs = [row] + ([pl.BlockSpec((1, D), lambda i: (0, 0))] if has_gain else [])
    args = (x,) + ((gain.reshape(1, D),) if has_gain else ())
    return pl.pallas_call(body, out_shape=jax.ShapeDtypeStruct((R, D), F32), grid=(R // tr,),
                          in_specs=in_specs, out_specs=row, name="norm_fwd",
                          compiler_params=_params(("parallel",)))(*args)


def _norm_bwd_call(x, gain, dy, eps, inv_n):
    R, D = x.shape
    tr = _row_tile(R, D)
    has_gain = gain is not None

    def body(*refs):
        if has_gain:
            x_ref, g_ref, dy_ref, dx_ref, dg_ref = refs
        else:
            x_ref, dy_ref, dx_ref = refs
        xv = x_ref[...]
        dyv = dy_ref[...]
        r = lax.rsqrt(jnp.sum(xv * xv, axis=-1, keepdims=True) * inv_n + eps)
        xh = xv * r
        dxh = dyv * g_ref[...] if has_gain else dyv
        dx_ref[...] = r * (dxh - xh * (inv_n * jnp.sum(dxh * xh, axis=-1, keepdims=True)))
        if has_gain:
            @pl.when(pl.program_id(0) == 0)
            def _():
                dg_ref[...] = jnp.zeros_like(dg_ref)

            dg_ref[...] += jnp.sum(dyv * xh, axis=0, keepdims=True)

    row = pl.BlockSpec((tr, D), lambda i: (i, 0))
    vec = pl.BlockSpec((1, D), lambda i: (0, 0))
    if has_gain:
        dx, dg = pl.pallas_call(
            body, out_shape=(jax.ShapeDtypeStruct((R, D), F32), jax.ShapeDtypeStruct((1, D), F32)),
            grid=(R // tr,), in_specs=[row, vec, row], out_specs=(row, vec), name="norm_bwd",
            compiler_params=_params(("arbitrary",)))(x, gain.reshape(1, D), dy)
        return dx, dg.reshape(D)
    dx = pl.pallas_call(body, out_shape=jax.ShapeDtypeStruct((R, D), F32), grid=(R // tr,),
                        in_specs=[row, row], out_specs=row, name="l2norm_bwd",
                        compiler_params=_params(("parallel",)))(x, dy)
    return dx, None


@functools.partial(jax.custom_vjp, nondiff_argnums=(2,))
def rms_norm(x, gain, eps):
    return _norm_fwd_call(x, gain, eps, 1.0 / x.shape[-1])


def _rms_norm_fwd(x, gain, eps):
    return rms_norm(x, gain, eps), (x, gain)


def _rms_norm_bwd(eps, saved, dy):
    x, gain = saved
    return _norm_bwd_call(x, gain, dy, eps, 1.0 / x.shape[-1])


rms_norm.defvjp(_rms_norm_fwd, _rms_norm_bwd)


@jax.custom_vjp
def l2_norm(x):
    return _norm_fwd_call(x, None, EPS, 1.0)


def _l2_norm_fwd(x):
    return l2_norm(x), x


def _l2_norm_bwd(x, dy):
    return (_norm_bwd_call(x, None, dy, EPS, 1.0)[0],)


l2_norm.defvjp(_l2_norm_fwd, _l2_norm_bwd)


def _attn_mask(tq, sk):
    q0 = pl.program_id(1) * tq
    qc = (q0 + lax.broadcasted_iota(jnp.int32, (tq, sk), 0)) // CHUNK
    kc = lax.broadcasted_iota(jnp.int32, (tq, sk), 1) // CHUNK
    return kc <= qc


def _attn_fwd_call(q, k, v, scale, causal):
    H, S, dk = q.shape
    Sk, dv = k.shape[1], v.shape[2]
    tq = _pick(S, (256, 128, 64))

    def body(q_ref, k_ref, v_ref, o_ref, lse_ref):
        s = lax.dot_general(q_ref[...].astype(BF16), k_ref[...].astype(BF16), NT,
                            preferred_element_type=F32) * scale
        if causal:
            s = jnp.where(_attn_mask(tq, Sk), s, -1e30)
        m = jnp.max(s, axis=-1, keepdims=True)
        p = jnp.exp(s - m)
        l = jnp.sum(p, axis=-1, keepdims=True)
        o = lax.dot_general(p.astype(BF16), v_ref[...].astype(BF16), NN, preferred_element_type=F32)
        o_ref[...] = o / l
        lse_ref[...] = m + jnp.log(l)

    return pl.pallas_call(
        body,
        out_shape=(jax.ShapeDtypeStruct((H, S, dv), F32), jax.ShapeDtypeStruct((H, S, 1), F32)),
        grid=(H, S // tq),
        in_specs=[pl.BlockSpec((None, tq, dk), lambda h, i: (h, i, 0)),
                  pl.BlockSpec((None, Sk, dk), lambda h, i: (h, 0, 0)),
                  pl.BlockSpec((None, Sk, dv), lambda h, i: (h, 0, 0))],
        out_specs=(pl.BlockSpec((None, tq, dv), lambda h, i: (h, i, 0)),
                   pl.BlockSpec((None, tq, 1), lambda h, i: (h, i, 0))),
        name="attn_fwd", compiler_params=_params(("parallel", "parallel")))(q, k, v)


def _attn_bwd_call(q, k, v, o, lse, do, scale, causal):
    H, S, dk = q.shape
    Sk, dv = k.shape[1], v.shape[2]
    tq = _pick(S, (256, 128, 64))

    def body(q_ref, k_ref, v_ref, o_ref, lse_ref, do_ref, dq_ref, dk_ref, dv_ref):
        qb = q_ref[...].astype(BF16)
        kb = k_ref[...].astype(BF16)
        dob = do_ref[...].astype(BF16)
        s = lax.dot_general(qb, kb, NT, preferred_element_type=F32) * scale
        if causal:
            s = jnp.where(_attn_mask(tq, Sk), s, -1e30)
        p = jnp.exp(s - lse_ref[...])
        dp = lax.dot_general(dob, v_ref[...].astype(BF16), NT, preferred_element_type=F32)
        delta = jnp.sum(do_ref[...] * o_ref[...], axis=-1, keepdims=True)
        ds = (p * (dp - delta) * scale).astype(BF16)
        dq_ref[...] = lax.dot_general(ds, kb, NN, preferred_element_type=F32)

        @pl.when(pl.program_id(1) == 0)
        def _():
            dk_ref[...] = jnp.zeros_like(dk_ref)
            dv_ref[...] = jnp.zeros_like(dv_ref)

        dk_ref[...] += lax.dot_general(ds, qb, TN, preferred_element_type=F32)
        dv_ref[...] += lax.dot_general(p.astype(BF16), dob, TN, preferred_element_type=F32)

    qs = pl.BlockSpec((None, tq, dk), lambda h, i: (h, i, 0))
    ks = pl.BlockSpec((None, Sk, dk), lambda h, i: (h, 0, 0))
    vs = pl.BlockSpec((None, Sk, dv), lambda h, i: (h, 0, 0))
    os_ = pl.BlockSpec((None, tq, dv), lambda h, i: (h, i, 0))
    ls = pl.BlockSpec((None, tq, 1), lambda h, i: (h, i, 0))
    return pl.pallas_call(
        body,
        out_shape=(jax.ShapeDtypeStruct(q.shape, F32), jax.ShapeDtypeStruct(k.shape, F32),
                   jax.ShapeDtypeStruct(v.shape, F32)),
        grid=(H, S // tq), in_specs=[qs, ks, vs, os_, ls, os_], out_specs=(qs, ks, vs),
        name="attn_bwd", compiler_params=_params(("parallel", "arbitrary")))(q, k, v, o, lse, do)


@functools.partial(jax.custom_vjp, nondiff_argnums=(3, 4))
def attention(q, k, v, scale, causal):
    return _attn_fwd_call(q, k, v, scale, causal)[0]


def _attention_fwd(q, k, v, scale, causal):
    o, lse = _attn_fwd_call(q, k, v, scale, causal)
    return o, (q, k, v, o, lse)


def _attention_bwd(scale, causal, saved, do):
    q, k, v, o, lse = saved
    return _attn_bwd_call(q, k, v, o, lse, do, scale, causal)


attention.defvjp(_attention_fwd, _attention_bwd)


def _hdot(a, b, dims=NN):
    return lax.dot_general(a, b, dims, precision=HIGHEST, preferred_element_type=F32)


def _dn_chunk(q, k, v, gcol, grow, bcol, state):
    c = q.shape[0]
    r_i = lax.broadcasted_iota(jnp.int32, (c, c), 0)
    c_i = lax.broadcasted_iota(jnp.int32, (c, c), 1)
    incl = c_i <= r_i
    strict = c_i < r_i
    g_cum_col = jnp.sum(jnp.where(incl, grow, 0.0), axis=1, keepdims=True)
    g_cum_row = jnp.sum(jnp.where(r_i <= c_i, gcol, 0.0), axis=0, keepdims=True)
    g_last = jnp.sum(grow, axis=1, keepdims=True)
    decay = jnp.where(incl, jnp.exp(jnp.where(incl, g_cum_col - g_cum_row, 0.0)), 0.0)
    qs = q * (q.shape[1] ** -0.5)
    kb = k * bcol
    x = -jnp.where(strict, _hdot(kb, k, NT) * decay, 0.0)
    eye = (r_i == c_i).astype(F32)
    t = eye + x
    steps = max(1, int(np.ceil(np.log2(c))) - 1)
    for _ in range(steps):
        x = _hdot(x, x)
        t = t + _hdot(t, x)
    e_col = jnp.exp(g_cum_col)
    u = _hdot(t, v * bcol)
    w = _hdot(t, kb * e_col)
    attn = _hdot(qs, k, NT) * decay
    v_new = u - _hdot(w, state)
    o = _hdot(qs * e_col, state) + _hdot(attn, v_new)
    k_dec = k * jnp.exp(g_last - g_cum_col)
    new_state = state * jnp.exp(g_last) + _hdot(k_dec, v_new, TN)
    return o, new_state


def _dn_specs(H, S, D, N):
    seq = pl.BlockSpec((None, CHUNK, D), lambda h, n: (h, n, 0))
    col = pl.BlockSpec((None, CHUNK, 1), lambda h, n: (h, n, 0))
    row = pl.BlockSpec((None, None, 1, CHUNK), lambda h, n: (h, n, 0, 0))
    st = pl.BlockSpec((None, None, D, D), lambda h, n: (h, n, 0, 0))
    return seq, col, row, st


def _dn_fwd_call(q, k, v, gcol, grow, bcol):
    H, S, D = q.shape
    N = S // CHUNK
    seq, col, row, st = _dn_specs(H, S, D, N)

    def body(q_ref, k_ref, v_ref, gc_ref, gr_ref, bc_ref, o_ref, st_ref, state):
        @pl.when(pl.program_id(1) == 0)
        def _():
            state[...] = jnp.zeros_like(state)

        s_in = state[...]
        st_ref[...] = s_in
        o, s_out = _dn_chunk(q_ref[...], k_ref[...], v_ref[...], gc_ref[...], gr_ref[...], bc_ref[...], s_in)
        o_ref[...] = o
        state[...] = s_out

    return pl.pallas_call(
        body,
        out_shape=(jax.ShapeDtypeStruct((H, S, D), F32), jax.ShapeDtypeStruct((H, N, D, D), F32)),
        grid=(H, N), in_specs=[seq, seq, seq, col, row, col], out_specs=(seq, st),
        scratch_shapes=[pltpu.VMEM((D, D), F32)], name="deltanet_fwd",
        compiler_params=_params(("parallel", "arbitrary")))(q, k, v, gcol, grow, bcol)


def _dn_bwd_call(q, k, v, gcol, grow, bcol, states, do):
    H, S, D = q.shape
    N = S // CHUNK
    seq = pl.BlockSpec((None, CHUNK, D), lambda h, n: (h, N - 1 - n, 0))
    col = pl.BlockSpec((None, CHUNK, 1), lambda h, n: (h, N - 1 - n, 0))
    row = pl.BlockSpec((None, None, 1, CHUNK), lambda h, n: (h, N - 1 - n, 0, 0))
    st = pl.BlockSpec((None, None, D, D), lambda h, n: (h, N - 1 - n, 0, 0))

    def body(q_ref, k_ref, v_ref, gc_ref, gr_ref, bc_ref, st_ref, do_ref,
             dq_ref, dk_ref, dv_ref, dgc_ref, dgr_ref, dbc_ref, dstate):
        @pl.when(pl.program_id(1) == 0)
        def _():
            dstate[...] = jnp.zeros_like(dstate)

        _, vjp = jax.vjp(_dn_chunk, q_ref[...], k_ref[...], v_ref[...], gc_ref[...], gr_ref[...],
                         bc_ref[...], st_ref[...])
        dq, dk, dv, dgc, dgr, dbc, ds_in = vjp((do_ref[...], dstate[...]))
        dq_ref[...] = dq
        dk_ref[...] = dk
        dv_ref[...] = dv
        dgc_ref[...] = dgc
        dgr_ref[...] = dgr
        dbc_ref[...] = dbc
        dstate[...] = ds_in

    sd = jax.ShapeDtypeStruct
    return pl.pallas_call(
        body,
        out_shape=(sd((H, S, D), F32), sd((H, S, D), F32), sd((H, S, D), F32),
                   sd((H, S, 1), F32), sd((H, N, 1, CHUNK), F32), sd((H, S, 1), F32)),
        grid=(H, N), in_specs=[seq, seq, seq, col, row, col, st, seq],
        out_specs=(seq, seq, seq, col, row, col),
        scratch_shapes=[pltpu.VMEM((D, D), F32)], name="deltanet_bwd",
        compiler_params=_params(("parallel", "arbitrary")))(q, k, v, gcol, grow, bcol, states, do)


@jax.custom_vjp
def delta_rule(q, k, v, gcol, grow, bcol):
    return _dn_fwd_call(q, k, v, gcol, grow, bcol)[0]


def _delta_rule_fwd(q, k, v, gcol, grow, bcol):
    o, states = _dn_fwd_call(q, k, v, gcol, grow, bcol)
    return o, (q, k, v, gcol, grow, bcol, states)


def _delta_rule_bwd(saved, do):
    return _dn_bwd_call(*saved, do)


delta_rule.defvjp(_delta_rule_fwd, _delta_rule_bwd)


def _shift_down(x, j):
    if j == 0:
        return x
    rows = lax.broadcasted_iota(jnp.int32, x.shape, 0)
    return jnp.where(rows >= j, pltpu.roll(x, j, 0), 0.0)


def _shift_up(x, j):
    if j == 0:
        return x
    s = x.shape[0]
    rows = lax.broadcasted_iota(jnp.int32, x.shape, 0)
    return jnp.where(rows < s - j, pltpu.roll(x, s - j, 0), 0.0)


def _conv(x, w):
    kw = w.shape[0]
    acc = x * w[kw - 1:kw, :]
    for kk in range(kw - 1):
        acc = acc + _shift_down(x, kw - 1 - kk) * w[kk:kk + 1, :]
    return acc


def _conv_t(d, w):
    kw = w.shape[0]
    acc = d * w[kw - 1:kw, :]
    for kk in range(kw - 1):
        acc = acc + _shift_up(d, kw - 1 - kk) * w[kk:kk + 1, :]
    return acc


def _conv_dw_rows(d, x, kw):
    return [jnp.sum(d * _shift_down(x, kw - 1 - kk), axis=0, keepdims=True) for kk in range(kw)]


def _silu(a):
    return a * jax.nn.sigmoid(a)


def _dsilu(a):
    s = jax.nn.sigmoid(a)
    return s * (1.0 + a * (1.0 - s))


def _conv_silu_fwd_call(x, w):
    S, C = x.shape
    kw = w.shape[0]
    tc = _pick(C, (256, 128))

    def body(x_ref, w_ref, y_ref):
        y_ref[...] = _silu(_conv(x_ref[...], w_ref[...]))

    xs = pl.BlockSpec((S, tc), lambda j: (0, j))
    ws = pl.BlockSpec((kw, tc), lambda j: (0, j))
    return pl.pallas_call(body, out_shape=jax.ShapeDtypeStruct((S, C), F32), grid=(C // tc,),
                          in_specs=[xs, ws], out_specs=xs, name="conv_silu_fwd",
                          compiler_params=_params(("parallel",)))(x, w)


def _conv_silu_bwd_call(x, w, dy):
    S, C = x.shape
    kw = w.shape[0]
    tc = _pick(C, (256, 128))

    def body(x_ref, w_ref, dy_ref, dx_ref, dw_ref):
        xv = x_ref[...]
        wv = w_ref[...]
        da = dy_ref[...] * _dsilu(_conv(xv, wv))
        dx_ref[...] = _conv_t(da, wv)
        for kk, row in enumerate(_conv_dw_rows(da, xv, kw)):
            dw_ref[kk:kk + 1, :] = row

    xs = pl.BlockSpec((S, tc), lambda j: (0, j))
    ws = pl.BlockSpec((kw, tc), lambda j: (0, j))
    return pl.pallas_call(
        body, out_shape=(jax.ShapeDtypeStruct((S, C), F32), jax.ShapeDtypeStruct((kw, C), F32)),
        grid=(C // tc,), in_specs=[xs, ws, xs], out_specs=(xs, ws), name="conv_silu_bwd",
        compiler_params=_params(("parallel",)))(x, w, dy)


@jax.custom_vjp
def conv_silu(x, w):
    return _conv_silu_fwd_call(x, w)


def _conv_silu_fwd(x, w):
    return conv_silu(x, w), (x, w)


def _conv_silu_bwd(saved, dy):
    return _conv_silu_bwd_call(*saved, dy)


conv_silu.defvjp(_conv_silu_fwd, _conv_silu_bwd)


def _conv_glu_fwd_call(x, w, b, tc):
    S, C2 = x.shape
    kw = w.shape[0]
    nb = C2 // (2 * tc)

    def body(x_ref, w_ref, b_ref, y_ref):
        a = _conv(x_ref[...], w_ref[...]) + b_ref[...]
        y_ref[...] = _silu(a[:, :tc]) * a[:, tc:]

    return pl.pallas_call(
        body, out_shape=jax.ShapeDtypeStruct((S, C2 // 2), F32), grid=(nb,),
        in_specs=[pl.BlockSpec((S, 2 * tc), lambda j: (0, j)), pl.BlockSpec((kw, 2 * tc), lambda j: (0, j)),
                  pl.BlockSpec((1, 2 * tc), lambda j: (0, j))],
        out_specs=pl.BlockSpec((S, tc), lambda j: (0, j)), name="conv_glu_fwd",
        compiler_params=_params(("parallel",)))(x, w, b)


def _conv_glu_bwd_call(x, w, b, dy, tc):
    S, C2 = x.shape
    kw = w.shape[0]
    nb = C2 // (2 * tc)

    def body(x_ref, w_ref, b_ref, dy_ref, dx_ref, dw_ref, db_ref):
        xv = x_ref[...]
        wv = w_ref[...]
        a = _conv(xv, wv) + b_ref[...]
        ag, au = a[:, :tc], a[:, tc:]
        dyv = dy_ref[...]
        da = jnp.concatenate([dyv * au * _dsilu(ag), dyv * _silu(ag)], axis=1)
        dx_ref[...] = _conv_t(da, wv)
        for kk, row in enumerate(_conv_dw_rows(da, xv, kw)):
            dw_ref[kk:kk + 1, :] = row
        db_ref[...] = jnp.sum(da, axis=0, keepdims=True)

    xs = pl.BlockSpec((S, 2 * tc), lambda j: (0, j))
    ws = pl.BlockSpec((kw, 2 * tc), lambda j: (0, j))
    bs = pl.BlockSpec((1, 2 * tc), lambda j: (0, j))
    return pl.pallas_call(
        body, out_shape=(jax.ShapeDtypeStruct((S, C2), F32), jax.ShapeDtypeStruct((kw, C2), F32),
                         jax.ShapeDtypeStruct((1, C2), F32)),
        grid=(nb,), in_specs=[xs, ws, bs, pl.BlockSpec((S, tc), lambda j: (0, j))], out_specs=(xs, ws, bs),
        name="conv_glu_bwd", compiler_params=_params(("parallel",)))(x, w, b, dy)


@functools.partial(jax.custom_vjp, nondiff_argnums=(3,))
def conv_glu(x, w, b, tc):
    return _conv_glu_fwd_call(x, w, b, tc)


def _conv_glu_fwd(x, w, b, tc):
    return conv_glu(x, w, b, tc), (x, w, b)


def _conv_glu_bwd(tc, saved, dy):
    return _conv_glu_bwd_call(*saved, dy, tc)


conv_glu.defvjp(_conv_glu_fwd, _conv_glu_bwd)


def _sqerr_call(y, t):
    S, D = y.shape
    tr = _row_tile(S, D)

    def body(y_ref, t_ref, l_ref, dy_ref):
        d = y_ref[...] - t_ref[...]
        dy_ref[...] = d * (1.0 / D)

        @pl.when(pl.program_id(0) == 0)
        def _():
            l_ref[...] = jnp.zeros_like(l_ref)

        l_ref[...] += jnp.full((8, LANES), 0.5 / D, F32) * jnp.sum(d * d)

    row = pl.BlockSpec((tr, D), lambda i: (i, 0))
    one = pl.BlockSpec((8, LANES), lambda i: (0, 0))
    return pl.pallas_call(
        body, out_shape=(jax.ShapeDtypeStruct((8, LANES), F32), jax.ShapeDtypeStruct((S, D), F32)),
        grid=(S // tr,), in_specs=[row, row], out_specs=(one, row), name="loss_head",
        compiler_params=_params(("arbitrary",)))(y, t)


@jax.custom_vjp
def sq_loss(y, t):
    return _sqerr_call(y, t)[0][0, 0]


def _sq_loss_fwd(y, t):
    l, dy = _sqerr_call(y, t)
    return l[0, 0], dy


def _sq_loss_bwd(dy, dl):
    return dy * dl, None


sq_loss.defvjp(_sq_loss_fwd, _sq_loss_bwd)


def adamw(w, g, m, v):
    shape = w.shape
    C = shape[-1]
    R = int(np.prod(shape[:-1]))
    tr = _pick(R, tuple(t for t in (4096, 2048, 1024, 512, 256, 128, 64, 32, 16, 8) if t * C * 4 <= (1 << 20)))
    c1 = 1.0 / (1.0 - ADAM_B1 ** ADAM_STEP)
    c2 = 1.0 / (1.0 - ADAM_B2 ** ADAM_STEP)

    def body(w_ref, g_ref, m_ref, v_ref, d_ref, nm_ref, nv_ref):
        gv = g_ref[...]
        nm = ADAM_B1 * m_ref[...] + (1.0 - ADAM_B1) * gv
        nv = ADAM_B2 * v_ref[...] + (1.0 - ADAM_B2) * (gv * gv)
        d_ref[...] = -ADAM_LR * ((nm * c1) / (jnp.sqrt(nv * c2) + ADAM_EPS) + ADAM_WD * w_ref[...])
        nm_ref[...] = nm
        nv_ref[...] = nv

    blk = pl.BlockSpec((tr, C), lambda i: (i, 0))
    out = jax.ShapeDtypeStruct((R, C), F32)
    d, nm, nv = pl.pallas_call(body, out_shape=(out, out, out), grid=(R // tr,), in_specs=[blk] * 4,
                               out_specs=(blk, blk, blk), name="adamw",
                               compiler_params=_params(("parallel",)))(
        w.reshape(R, C), g.reshape(R, C), m.reshape(R, C), v.reshape(R, C))
    return d.reshape(shape), nm.reshape(shape), nv.reshape(shape)


def _place():
    x, y, c = lax.axis_index("x"), lax.axis_index("y"), lax.axis_index("c")
    chips = [(1 - x, y), (x, 1 - y), (1 - x, 1 - y)]
    return x, y, c, chips


def small_allgather(buf):
    m_per, n = buf.shape

    def body(x_ref, out_ref, send_sems, recv_sems, local_sem):
        x, y, c, chips = _place()
        me, sibling = (x, y, c), (x, y, 1 - c)

        def rows(px, py, pc):
            return out_ref.at[4 * px + 2 * py + pc]

        def copy(k, block, to, src=None):
            return pltpu.make_async_remote_copy(
                src_ref=rows(*block) if src is None else src, dst_ref=rows(*block),
                send_sem=send_sems.at[k], recv_sem=recv_sems.at[k], device_id=to, device_id_type=MESH)

        mine = pltpu.make_async_copy(x_ref, rows(*me), local_sem)
        mine.start()
        first = [copy(0, me, sibling, src=x_ref)]
        first += [copy(1 + j, me, (*chip, c), src=x_ref) for j, chip in enumerate(chips)]
        for cp in first:
            cp.start()
        passed = [copy(4 + j, (*chip, c), sibling) for j, chip in enumerate(chips)]
        for j, chip in enumerate(chips):
            copy(1 + j, (*chip, c), me).wait_recv()
            passed[j].start()
        copy(0, sibling, me).wait_recv()
        for j, chip in enumerate(chips):
            copy(4 + j, (*chip, 1 - c), me).wait_recv()
        for cp in first + passed:
            cp.wait_send()
        mine.wait()

    return pl.pallas_call(
        body, out_shape=jax.ShapeDtypeStruct((N_DEV, m_per, n), buf.dtype),
        in_specs=[pl.BlockSpec(memory_space=pltpu.VMEM)], out_specs=pl.BlockSpec(memory_space=pltpu.VMEM),
        scratch_shapes=[pltpu.SemaphoreType.DMA((7,)), pltpu.SemaphoreType.DMA((7,)), pltpu.SemaphoreType.DMA],
        name="small_allgather", compiler_params=pltpu.CompilerParams(vmem_limit_bytes=VMEM_LIMIT))(buf)


def sum_blocks(g):
    n, R, C = g.shape
    tr = _pick(R, (512, 256, 128, 64, 32, 16, 8))

    def body(g_ref, o_ref):
        acc = g_ref[0].astype(F32)
        for d in range(1, n):
            acc = acc + g_ref[d].astype(F32)
        o_ref[...] = acc

    return pl.pallas_call(body, out_shape=jax.ShapeDtypeStruct((R, C), F32), grid=(R // tr,),
                          in_specs=[pl.BlockSpec((n, tr, C), lambda i: (0, i, 0))],
                          out_specs=pl.BlockSpec((tr, C), lambda i: (i, 0)), name="sum_blocks",
                          compiler_params=_params(("parallel",)))(g)


def _hbm_specs(n):
    return [pl.BlockSpec(memory_space=pl.ANY)] * n


def gather_weights(shards):
    n = len(shards)

    def body(*refs):
        src, out = refs[:n], refs[n:2 * n]
        send_sems, recv_sems, local_sems = refs[2 * n:]
        x, y, c, chips = _place()
        sibling = (x, y, 1 - c)
        my_chip = 2 * x + y
        locals_, sends, passes = [], [], []
        for t in range(n):
            half = src[t].shape[0] // 2
            lc = pltpu.make_async_copy(src[t], out[t].at[my_chip], local_sems.at[t])
            lc.start()
            locals_.append(lc)
            for j, chip in enumerate(chips):
                cp = pltpu.make_async_remote_copy(
                    src_ref=src[t].at[pl.ds(c * half, half)], dst_ref=out[t].at[my_chip, pl.ds(c * half, half)],
                    send_sem=send_sems.at[t, j], recv_sem=recv_sems.at[t, j],
                    device_id=(*chip, c), device_id_type=MESH)
                cp.start()
                sends.append(cp)
        for t in range(n):
            half = src[t].shape[0] // 2
            for j, chip in enumerate(chips):
                landed = out[t].at[2 * chip[0] + chip[1], pl.ds(c * half, half)]
                pltpu.make_async_remote_copy(
                    src_ref=landed, dst_ref=landed, send_sem=send_sems.at[t, j], recv_sem=recv_sems.at[t, j],
                    device_id=(*chip, c), device_id_type=MESH).wait_recv()
                fw = pltpu.make_async_remote_copy(
                    src_ref=landed, dst_ref=landed, send_sem=send_sems.at[t, 3 + j], recv_sem=recv_sems.at[t, 3 + j],
                    device_id=sibling, device_id_type=MESH)
                fw.start()
                passes.append(fw)
        for t in range(n):
            half = src[t].shape[0] // 2
            for j, chip in enumerate(chips):
                other = out[t].at[2 * chip[0] + chip[1], pl.ds((1 - c) * half, half)]
                pltpu.make_async_remote_copy(
                    src_ref=other, dst_ref=other, send_sem=send_sems.at[t, 3 + j], recv_sem=recv_sems.at[t, 3 + j],
                    device_id=sibling, device_id_type=MESH).wait_recv()
        for cp in sends + passes:
            cp.wait_send()
        for lc in locals_:
            lc.wait()

    out_shape = tuple(jax.ShapeDtypeStruct((N_CHIPS,) + s.shape, s.dtype) for s in shards)
    return pl.pallas_call(
        body, out_shape=out_shape, in_specs=_hbm_specs(n), out_specs=tuple(_hbm_specs(n)),
        scratch_shapes=[pltpu.SemaphoreType.DMA((n, 6)), pltpu.SemaphoreType.DMA((n, 6)),
                        pltpu.SemaphoreType.DMA((n,))],
        name="gather_weights", compiler_params=pltpu.CompilerParams(has_side_effects=True))(*shards)


def sibling_exchange_halves(grads):
    n = len(grads)

    def body(*refs):
        src, out = refs[:n], refs[n:2 * n]
        send_sems, recv_sems = refs[2 * n:]
        x, y, c, _ = _place()
        sibling = (x, y, 1 - c)
        cps = []
        for t in range(n):
            half = src[t].shape[1] // 2
            cp = pltpu.make_async_remote_copy(
                src_ref=src[t].at[:, pl.ds((1 - c) * half, half)], dst_ref=out[t],
                send_sem=send_sems.at[t], recv_sem=recv_sems.at[t], device_id=sibling, device_id_type=MESH)
            cp.start()
            cps.append(cp)
        for cp in cps:
            cp.wait()

    out_shape = tuple(jax.ShapeDtypeStruct((s.shape[0], s.shape[1] // 2, s.shape[2]), s.dtype) for s in grads)
    return pl.pallas_call(
        body, out_shape=out_shape, in_specs=_hbm_specs(n), out_specs=tuple(_hbm_specs(n)),
        scratch_shapes=[pltpu.SemaphoreType.DMA((n,)), pltpu.SemaphoreType.DMA((n,))],
        name="sibling_exchange_halves", compiler_params=pltpu.CompilerParams(has_side_effects=True))(*grads)


def pair_sum(g, recv, c_idx):
    G, R, C = g.shape
    half = R // 2
    tr = _pick(half, tuple(t for t in (1024, 512, 256, 128, 64, 32, 16) if t * C * 2 <= (1 << 20)))
    g4 = g.reshape(G, 2, half, C)

    def body(c_ref, g_ref, r_ref, o_ref):
        o_ref[...] = (g_ref[...].astype(F32) + r_ref[...].astype(F32)).astype(o_ref.dtype)

    grid_spec = pltpu.PrefetchScalarGridSpec(
        num_scalar_prefetch=1, grid=(G, half // tr),
        in_specs=[pl.BlockSpec((None, None, tr, C), lambda i, j, c_ref: (i, c_ref[0], j, 0)),
                  pl.BlockSpec((None, tr, C), lambda i, j, c_ref: (i, j, 0))],
        out_specs=pl.BlockSpec((None, tr, C), lambda i, j, c_ref: (i, j, 0)))
    return pl.pallas_call(body, out_shape=jax.ShapeDtypeStruct((G, half, C), g.dtype), grid_spec=grid_spec,
                          name="pair_sum", compiler_params=_params(("parallel", "parallel")))(c_idx, g4, recv)


def chip_exchange(parts):
    n = len(parts)

    def body(*refs):
        src, out = refs[:n], refs[n:2 * n]
        send_sems, recv_sems, local_sems = refs[2 * n:]
        x, y, c, chips = _place()
        my_chip = 2 * x + y
        cps, locals_ = [], []
        for t in range(n):
            lc = pltpu.make_async_copy(src[t].at[my_chip], out[t].at[3], local_sems.at[t])
            lc.start()
            locals_.append(lc)
            for j, chip in enumerate(chips):
                cp = pltpu.make_async_remote_copy(
                    src_ref=src[t].at[2 * chip[0] + chip[1]], dst_ref=out[t].at[j],
                    send_sem=send_sems.at[t, j], recv_sem=recv_sems.at[t, j],
                    device_id=(*chip, c), device_id_type=MESH)
                cp.start()
                cps.append(cp)
        for cp in cps:
            cp.wait()
        for lc in locals_:
            lc.wait()

    out_shape = tuple(jax.ShapeDtypeStruct(s.shape, s.dtype) for s in parts)
    return pl.pallas_call(
        body, out_shape=out_shape, in_specs=_hbm_specs(n), out_specs=tuple(_hbm_specs(n)),
        scratch_shapes=[pltpu.SemaphoreType.DMA((n, 3)), pltpu.SemaphoreType.DMA((n, 3)),
                        pltpu.SemaphoreType.DMA((n,))],
        name="chip_exchange", compiler_params=pltpu.CompilerParams(has_side_effects=True))(*parts)


def sum_slots(p):
    G, H, C = p.shape
    tr = _pick(H, tuple(t for t in (1024, 512, 256, 128, 64, 32, 16) if t * C * 4 <= (1 << 20)))

    def body(p_ref, o_ref):
        acc = p_ref[0].astype(F32)
        for s in range(1, G):
            acc = acc + p_ref[s].astype(F32)
        o_ref[...] = acc

    return pl.pallas_call(body, out_shape=jax.ShapeDtypeStruct((H, C), F32), grid=(H // tr,),
                          in_specs=[pl.BlockSpec((G, tr, C), lambda i: (0, i, 0))],
                          out_specs=pl.BlockSpec((tr, C), lambda i: (i, 0)), name="sum_slots",
                          compiler_params=_params(("parallel",)))(p)


def sibling_join_halves(halves):
    n = len(halves)

    def body(*refs):
        src, out = refs[:n], refs[n:2 * n]
        send_sems, recv_sems, local_sems = refs[2 * n:]
        x, y, c, _ = _place()
        sibling = (x, y, 1 - c)
        cps, locals_ = [], []
        for t in range(n):
            h = src[t].shape[0]
            mine = out[t].at[pl.ds(c * h, h)]
            lc = pltpu.make_async_copy(src[t], mine, local_sems.at[t])
            lc.start()
            locals_.append(lc)
            cp = pltpu.make_async_remote_copy(
                src_ref=src[t], dst_ref=mine, send_sem=send_sems.at[t], recv_sem=recv_sems.at[t],
                device_id=sibling, device_id_type=MESH)
            cp.start()
            cps.append(cp)
        for t in range(n):
            h = src[t].shape[0]
            theirs = out[t].at[pl.ds((1 - c) * h, h)]
            pltpu.make_async_remote_copy(
                src_ref=src[t], dst_ref=theirs, send_sem=send_sems.at[t], recv_sem=recv_sems.at[t],
                device_id=sibling, device_id_type=MESH).wait_recv()
        for cp in cps:
            cp.wait_send()
        for lc in locals_:
            lc.wait()

    out_shape = tuple(jax.ShapeDtypeStruct((2 * s.shape[0], s.shape[1]), s.dtype) for s in halves)
    return pl.pallas_call(
        body, out_shape=out_shape, in_specs=_hbm_specs(n), out_specs=tuple(_hbm_specs(n)),
        scratch_shapes=[pltpu.SemaphoreType.DMA((n,)), pltpu.SemaphoreType.DMA((n,)),
                        pltpu.SemaphoreType.DMA((n,))],
        name="sibling_join_halves", compiler_params=pltpu.CompilerParams(has_side_effects=True))(*halves)


def reduce_scatter(grads, c_idx):
    recv = sibling_exchange_halves(grads)
    parts = [pair_sum(g, r, c_idx) for g, r in zip(grads, recv)]
    slots = chip_exchange(parts)
    halves = [sum_slots(s) for s in slots]
    return sibling_join_halves(halves)


BIG = ("w_in", "mla_w_qb", "mla_w_kvb", "w_out", "xa_wq", "xa_wk", "xa_wv", "xa_wo", "ffn_w_up", "ffn_w_down")
COL_SHARDED = ("w_in", "mla_w_qb", "mla_w_kvb", "ffn_w_up")
SMALL_REPL = ("norm_mix", "dn_a_log", "dn_dt_bias", "dn_out_norm", "mla_q_norm", "mla_kv_norm", "mem_norm",
              "norm_xattn", "norm_ffn", "ffn_conv_bias", "norm_final")
SMALL_SHARDED = ("dn_conv", "ffn_conv")
WEIGHTS = ("norm_mix", "w_in", "dn_conv", "dn_a_log", "dn_dt_bias", "dn_out_norm", "mla_q_norm", "mla_w_qb",
           "mla_kv_norm", "mla_w_kvb", "w_out", "mem_norm", "norm_xattn", "xa_wq", "xa_wk", "xa_wv", "xa_wo",
           "norm_ffn", "ffn_w_up", "ffn_conv", "ffn_conv_bias", "ffn_w_down", "norm_final")


def _pad_cols(a, cp):
    c = a.shape[-1]
    if c == cp:
        return a
    return jnp.pad(a, [(0, 0)] * (a.ndim - 1) + [(0, cp - c)])


def _pack(arrs):
    flat = jnp.concatenate([a.reshape(-1).astype(F32) for a in arrs])
    rows = -(-flat.shape[0] // LANES)
    rows = -(-rows // 8) * 8
    return jnp.pad(flat, (0, rows * LANES - flat.shape[0])).reshape(rows, LANES)


def _unpack(buf, like):
    flat = buf.reshape(-1)
    out, off = [], 0
    for a in like:
        n = int(np.prod(a.shape))
        out.append(flat[off:off + n].reshape(a.shape))
        off += n
    return out


def _heads(t, h):
    s = t.shape[0]
    return jnp.transpose(t.reshape(s, h, t.shape[1] // h), (1, 0, 2))


def _unheads(t):
    h, s, d = t.shape
    return jnp.transpose(t, (1, 0, 2)).reshape(s, h * d)


def _rope(t, cos, sin):
    t1, t2 = jnp.split(t, 2, axis=-1)
    return jnp.concatenate([t1 * cos - t2 * sin, t2 * cos + t1 * sin], axis=-1)


def _forward_loss(p, x, mem, cos, sin, target, dims):
    S, D = x.shape
    depth, dn_h, mla_h, d_ff, c_in, c_in_pad, ffn_tn = dims
    dn_w = dn_h * DN_HEAD_DIM
    n_chunks = S // CHUNK
    mem_n = rms_norm(mem, p["mem_norm"], EPS)
    h = x
    ffn_blocks = (2 * d_ff) // ffn_tn
    half_blocks = ffn_blocks // 2

    def ffn_perm(nb):
        return jnp.where(nb < half_blocks, 2 * nb, 2 * (nb - half_blocks) + 1)

    def ffn_layout(a):
        k = a.shape[0]
        return jnp.transpose(a.reshape(k, 2, half_blocks, ffn_tn), (0, 2, 1, 3)).reshape(k, 2 * d_ff)

    for l in range(depth):
        u = rms_norm(h, p["norm_mix"][l], EPS)
        projp = linear(u, p["w_in"][l])
        proj = jnp.concatenate([projp[:, g * c_in_pad:g * c_in_pad + c_in] for g in range(N_CHIPS)], axis=1)
        o0 = 3 * dn_w
        dz = proj[:, o0:o0 + dn_w]
        db = proj[:, o0 + dn_w:o0 + dn_w + dn_h]
        da = proj[:, o0 + dn_w + dn_h:o0 + dn_w + 2 * dn_h]
        o1 = o0 + dn_w + 2 * dn_h
        mq = proj[:, o1:o1 + MLA_Q_RANK]
        mkv = proj[:, o1 + MLA_Q_RANK:]
        qkv = conv_silu(proj[:, :o0], p["dn_conv"][l])
        qd = l2_norm(qkv[:, :dn_w].reshape(S * dn_h, DN_HEAD_DIM)).reshape(S, dn_w)
        kd = l2_norm(qkv[:, dn_w:2 * dn_w].reshape(S * dn_h, DN_HEAD_DIM)).reshape(S, dn_w)
        vd = qkv[:, 2 * dn_w:]
        beta = jax.nn.sigmoid(db)
        g = -jnp.exp(p["dn_a_log"][l]) * jax.nn.softplus(da + p["dn_dt_bias"][l])
        g_t, beta_t = g.T, beta.T
        o_dn = delta_rule(_heads(qd, dn_h), _heads(kd, dn_h), _heads(vd, dn_h), g_t[:, :, None],
                          g_t.reshape(dn_h, n_chunks, 1, CHUNK), beta_t[:, :, None])
        o_dn = _unheads(o_dn)
        o_dn = rms_norm(o_dn.reshape(S * dn_h, DN_HEAD_DIM), p["dn_out_norm"][l], EPS).reshape(S, dn_w)
        o_dn = o_dn * jax.nn.silu(dz)
        qf = linear(rms_norm(mq, p["mla_q_norm"][l], EPS), p["mla_w_qb"][l]).reshape(S, mla_h, MLA_NOPE + MLA_ROPE)
        q_pe = _rope(qf[..., MLA_NOPE:], cos[:, None, :], sin[:, None, :])
        k_pe = _rope(mkv[:, MLA_KV_RANK:], cos, sin)
        kv = linear(rms_norm(mkv[:, :MLA_KV_RANK], p["mla_kv_norm"][l], EPS), p["mla_w_kvb"][l])
        kv = kv.reshape(S, mla_h, MLA_NOPE + MLA_V)
        qa = jnp.transpose(jnp.concatenate([qf[..., :MLA_NOPE], q_pe], axis=-1), (1, 0, 2))
        ka = jnp.transpose(jnp.concatenate(
            [kv[..., :MLA_NOPE], jnp.broadcast_to(k_pe[:, None, :], (S, mla_h, MLA_ROPE))], axis=-1), (1, 0, 2))
        va = jnp.transpose(kv[..., MLA_NOPE:], (1, 0, 2))
        o_mla = _unheads(attention(qa, ka, va, (MLA_NOPE + MLA_ROPE) ** -0.5, True))
        h = linear(jnp.concatenate([o_dn, o_mla], axis=-1), p["w_out"][l], res=h)
        hn = rms_norm(h, p["norm_xattn"][l], EPS)
        xq = _heads(linear(hn, p["xa_wq"][l]), XA_HEADS)
        xk = _heads(linear(mem_n, p["xa_wk"][l]), XA_HEADS)
        xv = _heads(linear(mem_n, p["xa_wv"][l]), XA_HEADS)
        xo = _unheads(attention(xq, xk, xv, (D // XA_HEADS) ** -0.5, False))
        h = linear(xo, p["xa_wo"][l], res=h)
        hn = rms_norm(h, p["norm_ffn"][l], EPS)
        pre = linear(hn, p["ffn_w_up"][l], tn=ffn_tn, colperm=ffn_perm)
        act = conv_glu(pre, ffn_layout(p["ffn_conv"][l]), ffn_layout(p["ffn_conv_bias"][l][None, :]), ffn_tn)
        h = linear(act, p["ffn_w_down"][l], res=h)
    y = rms_norm(h, p["norm_final"], EPS)
    return sq_loss(y, target)


def kernel(x, mem, positions, norm_mix, w_in, dn_conv, dn_a_log, dn_dt_bias, dn_out_norm, mla_q_norm, mla_w_qb, mla_kv_norm, mla_w_kvb, w_out, mem_norm, norm_xattn, xa_wq, xa_wk, xa_wv, xa_wo, norm_ffn, ffn_w_up, ffn_conv, ffn_conv_bias, ffn_w_down, norm_final, loss_target, m_norm_mix, m_w_in, m_dn_conv, m_dn_a_log, m_dn_dt_bias, m_dn_out_norm, m_mla_q_norm, m_mla_w_qb, m_mla_kv_norm, m_mla_w_kvb, m_w_out, m_mem_norm, m_norm_xattn, m_xa_wq, m_xa_wk, m_xa_wv, m_xa_wo, m_norm_ffn, m_ffn_w_up, m_ffn_conv, m_ffn_conv_bias, m_ffn_w_down, m_norm_final, v_norm_mix, v_w_in, v_dn_conv, v_dn_a_log, v_dn_dt_bias, v_dn_out_norm, v_mla_q_norm, v_mla_w_qb, v_mla_kv_norm, v_mla_w_kvb, v_w_out, v_mem_norm, v_norm_xattn, v_xa_wq, v_xa_wk, v_xa_wv, v_xa_wo, v_norm_ffn, v_ffn_w_up, v_ffn_conv, v_ffn_conv_bias, v_ffn_w_down, v_norm_final):
    w = dict(norm_mix=norm_mix, w_in=w_in, dn_conv=dn_conv, dn_a_log=dn_a_log, dn_dt_bias=dn_dt_bias, dn_out_norm=dn_out_norm, mla_q_norm=mla_q_norm, mla_w_qb=mla_w_qb, mla_kv_norm=mla_kv_norm, mla_w_kvb=mla_w_kvb, w_out=w_out, mem_norm=mem_norm, norm_xattn=norm_xattn, xa_wq=xa_wq, xa_wk=xa_wk, xa_wv=xa_wv, xa_wo=xa_wo, norm_ffn=norm_ffn, ffn_w_up=ffn_w_up, ffn_conv=ffn_conv, ffn_conv_bias=ffn_conv_bias, ffn_w_down=ffn_w_down, norm_final=norm_final)
    m = dict(norm_mix=m_norm_mix, w_in=m_w_in, dn_conv=m_dn_conv, dn_a_log=m_dn_a_log, dn_dt_bias=m_dn_dt_bias, dn_out_norm=m_dn_out_norm, mla_q_norm=m_mla_q_norm, mla_w_qb=m_mla_w_qb, mla_kv_norm=m_mla_kv_norm, mla_w_kvb=m_mla_w_kvb, w_out=m_w_out, mem_norm=m_mem_norm, norm_xattn=m_norm_xattn, xa_wq=m_xa_wq, xa_wk=m_xa_wk, xa_wv=m_xa_wv, xa_wo=m_xa_wo, norm_ffn=m_norm_ffn, ffn_w_up=m_ffn_w_up, ffn_conv=m_ffn_conv, ffn_conv_bias=m_ffn_conv_bias, ffn_w_down=m_ffn_w_down, norm_final=m_norm_final)
    v = dict(norm_mix=v_norm_mix, w_in=v_w_in, dn_conv=v_dn_conv, dn_a_log=v_dn_a_log, dn_dt_bias=v_dn_dt_bias, dn_out_norm=v_dn_out_norm, mla_q_norm=v_mla_q_norm, mla_w_qb=v_mla_w_qb, mla_kv_norm=v_mla_kv_norm, mla_w_kvb=v_mla_w_kvb, w_out=v_w_out, mem_norm=v_mem_norm, norm_xattn=v_norm_xattn, xa_wq=v_xa_wq, xa_wk=v_xa_wk, xa_wv=v_xa_wv, xa_wo=v_xa_wo, norm_ffn=v_norm_ffn, ffn_w_up=v_ffn_w_up, ffn_conv=v_ffn_conv, ffn_conv_bias=v_ffn_conv_bias, ffn_w_down=v_ffn_w_down, norm_final=v_norm_final)

    S, D = x.shape[1], x.shape[2]
    depth = w_in.shape[0]
    dn_h = (D // 2) // DN_HEAD_DIM
    mla_h = (D - dn_h * DN_HEAD_DIM) // MLA_V
    d_ff = ffn_w_down.shape[1] * N_CHIPS
    c_in = w_in.shape[2]
    c_in_pad = -(-c_in // LANES) * LANES
    ffn_tn = _pick(ffn_w_up.shape[2], (256, 128))
    dims = (depth, dn_h, mla_h, d_ff, c_in, c_in_pad, ffn_tn)
    chip = 2 * lax.axis_index("x") + lax.axis_index("y")
    c_idx = lax.axis_index("c").astype(jnp.int32).reshape(1)

    gathered = {n: [] for n in BIG}
    for l in range(depth):
        shards = []
        for n in BIG:
            s = w[n][l].astype(BF16)
            if n == "w_in":
                s = _pad_cols(s, c_in_pad)
            shards.append(s)
        for n, full in zip(BIG, gather_weights(shards)):
            if n not in COL_SHARDED:
                full = full.reshape(1, N_CHIPS * full.shape[1], full.shape[2])
            gathered[n].append(full)
    small_sh = small_allgather(_pack([w[n] for n in SMALL_SHARDED]))
    params = dict(gathered)
    for n, parts in zip(SMALL_SHARDED, zip(*[_unpack(small_sh[2 * j], [w[k] for k in SMALL_SHARDED])
                                              for j in range(N_CHIPS)])):
        params[n] = jnp.concatenate(parts, axis=-1)
    for n in SMALL_REPL:
        params[n] = w[n]

    inv = ROPE_BASE ** (-jnp.arange(0, MLA_ROPE, 2, dtype=F32) / MLA_ROPE)
    ang = positions[0].astype(F32)[:, None] * inv
    cos, sin = jnp.cos(ang), jnp.sin(ang)
    loss_fn = lambda p, xx: _forward_loss(p, xx, mem[0], cos, sin, loss_target[0], dims)
    loss_local, (gp, gx) = jax.value_and_grad(loss_fn, argnums=(0, 1))(params, x[0])

    small_names = SMALL_REPL + SMALL_SHARDED
    packed = _pack([loss_local.reshape(1)] + [gp[n] for n in small_names])
    summed = sum_blocks(small_allgather(packed))
    unpacked = _unpack(summed, [loss_local.reshape(1)] + [gp[n] for n in small_names])
    loss = unpacked[0][0]
    grads = dict(zip(small_names, unpacked[1:]))
    for n in SMALL_SHARDED:
        cs = w[n].shape[-1]
        grads[n] = lax.dynamic_slice_in_dim(grads[n], chip * cs, cs, axis=-1)

    big_grads = {n: [] for n in BIG}
    for l in range(depth):
        gl = []
        for n in BIG:
            g = gp[n][l]
            if n not in COL_SHARDED:
                g = g.reshape(N_CHIPS, g.shape[1] // N_CHIPS, g.shape[2])
            gl.append(g)
        for n, r in zip(BIG, reduce_scatter(gl, c_idx)):
            if n == "w_in":
                r = r[:, :c_in]
            big_grads[n].append(r)
    for n in BIG:
        grads[n] = jnp.stack(big_grads[n])

    delta, new_m, new_v = {}, {}, {}
    for n in BIG:
        delta[n], new_m[n], new_v[n] = adamw(w[n], grads[n], m[n], v[n])
    sw, sg, sm, sv = (_pack([d[n] for n in small_names]) for d in (w, grads, m, v))
    sd, snm, snv = adamw(sw, sg, sm, sv)
    like = [w[n] for n in small_names]
    for d, buf in ((delta, sd), (new_m, snm), (new_v, snv)):
        for n, a in zip(small_names, _unpack(buf, like)):
            d[n] = a

    return (loss, gx[None], *[grads[n] for n in WEIGHTS], *[delta[n] for n in WEIGHTS],
            *[new_m[n] for n in WEIGHTS], *[new_v[n] for n in WEIGHTS])
```

```python
import functools

import numpy as np
import jax
import jax.numpy as jnp
from jax import lax
from jax.experimental import pallas as pl
from jax.experimental.pallas import tpu as pltpu

F32 = jnp.float32
BF16 = jnp.bfloat16
MESH = pl.DeviceIdType.MESH
HIGHEST = lax.Precision.HIGHEST

LANES = 128
VMEM_LIMIT = 56 * 1024 * 1024
N_CHIPS = 4
N_DEV = 8

CHUNK = 64
DN_HEAD_DIM = 128
MLA_NOPE, MLA_ROPE, MLA_V = 128, 64, 128
MLA_Q_RANK, MLA_KV_RANK = 512, 256
XA_HEADS = 4
ROPE_BASE = 10000.0
EPS = 1e-6

ADAM_LR, ADAM_B1, ADAM_B2, ADAM_EPS, ADAM_WD, ADAM_STEP = 0.001, 0.9, 0.999, 1e-08, 0.01, 10

NN = (((1,), (0,)), ((), ()))
NT = (((1,), (1,)), ((), ()))
TN = (((0,), (0,)), ((), ()))


def _pick(dim, cands):
    for c in cands:
        if c <= dim and dim % c == 0:
            return c
    return dim


def _params(sem=None):
    return pltpu.CompilerParams(dimension_semantics=sem, vmem_limit_bytes=VMEM_LIMIT)


def _mm_call(name, a, b, res, dims, grid, a_spec, b_spec, o_spec, out_shape, acc_shape):
    nk = grid[3]
    has_res = res is not None

    def body(*refs):
        if has_res:
            a_ref, b_ref, r_ref, o_ref, acc = refs
        else:
            a_ref, b_ref, o_ref, acc = refs
        kk = pl.program_id(3)

        @pl.when(kk == 0)
        def _():
            acc[...] = jnp.zeros_like(acc)

        acc[...] += lax.dot_general(a_ref[...].astype(BF16), b_ref[...].astype(BF16), dims,
                                    preferred_element_type=F32)

        @pl.when(kk == nk - 1)
        def _():
            r = acc[...]
            if has_res:
                r = r + r_ref[...]
            o_ref[...] = r.astype(o_ref.dtype)

    in_specs = [a_spec, b_spec] + ([o_spec] if has_res else [])
    args = (a, b) + ((res,) if has_res else ())
    return pl.pallas_call(
        body, out_shape=out_shape, grid=grid, in_specs=in_specs, out_specs=o_spec,
        scratch_shapes=[pltpu.VMEM(acc_shape, F32)], name=name,
        compiler_params=_params(("parallel", "parallel", "parallel", "arbitrary")),
    )(*args)


_TM = (1024, 512, 256, 128, 64, 32, 16, 8)
_TN = (1024, 768, 640, 512, 384, 256, 128)


def _ident(nb):
    return nb


def mm_nn_raw(a, w, res=None, tn=None, colperm=_ident, name="mm_nn"):
    M, K = a.shape
    G, _, C = w.shape
    tm = _pick(M, _TM)
    tn = tn or _pick(C, _TN)
    tk = K if K <= 2048 else _pick(K, (1408, 1024, 512, 256, 128))
    nj = C // tn
    grid = (G, M // tm, nj, K // tk)
    return _mm_call(
        name, a, w, res, NN, grid,
        pl.BlockSpec((tm, tk), lambda g, i, j, k: (i, k)),
        pl.BlockSpec((None, tk, tn), lambda g, i, j, k: (g, k, j)),
        pl.BlockSpec((tm, tn), lambda g, i, j, k: (i, colperm(g * nj + j))),
        jax.ShapeDtypeStruct((M, G * C), F32), (tm, tn))


def mm_nt_raw(d, w, tn=None, colperm=_ident, name="mm_nt"):
    M = d.shape[0]
    G, K, C = w.shape
    tm = _pick(M, _TM)
    tko = _pick(K, _TN)
    tc = tn or _pick(C, (2048, 1408, 1280, 1024, 512, 384, 256, 128))
    ncb = C // tc
    grid = (1, M // tm, K // tko, G * ncb)
    return _mm_call(
        name, d, w, None, NT, grid,
        pl.BlockSpec((tm, tc), lambda g, i, j, k: (i, colperm(k))),
        pl.BlockSpec((None, tko, tc), lambda g, i, j, k: (k // ncb, j, k % ncb)),
        pl.BlockSpec((tm, tko), lambda g, i, j, k: (i, j)),
        jax.ShapeDtypeStruct((M, K), F32), (tm, tko))


def mm_tn_raw(at, d, G, tn=None, colperm=_ident, name="mm_tn"):
    K, M = at.shape
    C = d.shape[1] // G
    tko = _pick(K, _TM)
    tn = tn or _pick(C, _TN)
    tm = M if M <= 2048 else _pick(M, (2048, 1024, 512, 256, 128))
    nj = C // tn
    grid = (G, K // tko, nj, M // tm)
    return _mm_call(
        name, at, d, None, NN, grid,
        pl.BlockSpec((tko, tm), lambda g, i, j, k: (i, k)),
        pl.BlockSpec((tm, tn), lambda g, i, j, k: (k, colperm(g * nj + j))),
        pl.BlockSpec((None, tko, tn), lambda g, i, j, k: (g, i, j)),
        jax.ShapeDtypeStruct((G, K, C), BF16), (tko, tn))


def _make_linear(tn, colperm, with_res):
    @jax.custom_vjp
    def lin(a, w, res):
        return mm_nn_raw(a.astype(BF16), w, res if with_res else None, tn=tn, colperm=colperm)

    def fwd(a, w, res):
        a16 = a.astype(BF16)
        return mm_nn_raw(a16, w, res if with_res else None, tn=tn, colperm=colperm), (a16, w)

    def bwd(saved, dout):
        a16, w = saved
        d16 = dout.astype(BF16)
        da = mm_nt_raw(d16, w, tn=tn, colperm=colperm)
        dw = mm_tn_raw(a16.T, d16, w.shape[0], tn=tn, colperm=colperm)
        return da, dw, (dout if with_res else None)

    lin.defvjp(fwd, bwd)
    return lin


def linear(a, w, res=None, tn=None, colperm=_ident):
    if res is None:
        return _make_linear(tn, colperm, False)(a, w, None)
    return _make_linear(tn, colperm, True)(a, w, res)


def _row_tile(R, D):
    return _pick(R, tuple(t for t in (2048, 1024, 512, 256, 128, 64, 32, 16, 8) if t * D * 4 <= (2 << 20)))


def _norm_fwd_call(x, gain, eps, inv_n):
    R, D = x.shape
    tr = _row_tile(R, D)
    has_gain = gain is not None

    def body(*refs):
        if has_gain:
            x_ref, g_ref, y_ref = refs
        else:
            x_ref, y_ref = refs
        xv = x_ref[...]
        r = lax.rsqrt(jnp.sum(xv * xv, axis=-1, keepdims=True) * inv_n + eps)
        y = xv * r
        if has_gain:
            y = y * g_ref[...]
        y_ref[...] = y

    row = pl.BlockSpec((tr, D), lambda i: (i, 0))
    in_specs = [row] + ([pl.BlockSpec((1, D), lambda i: (0, 0))] if has_gain else [])
    args = (x,) + ((gain.reshape(1, D),) if has_gain else ())
    return pl.pallas_call(body, out_shape=jax.ShapeDtypeStruct((R, D), F32), grid=(R // tr,),
                          in_specs=in_specs, out_specs=row, name="norm_fwd",
                          compiler_params=_params(("parallel",)))(*args)


def _norm_bwd_call(x, gain, dy, eps, inv_n):
    R, D = x.shape
    tr = _row_tile(R, D)
    has_gain = gain is not None

    def body(*refs):
        if has_gain:
            x_ref, g_ref, dy_ref, dx_ref, dg_ref = refs
        else:
            x_ref, dy_ref, dx_ref = refs
        xv = x_ref[...]
        dyv = dy_ref[...]
        r = lax.rsqrt(jnp.sum(xv * xv, axis=-1, keepdims=True) * inv_n + eps)
        xh = xv * r
        dxh = dyv * g_ref[...] if has_gain else dyv
        dx_ref[...] = r * (dxh - xh * (inv_n * jnp.sum(dxh * xh, axis=-1, keepdims=True)))
        if has_gain:
            @pl.when(pl.program_id(0) == 0)
            def _():
                dg_ref[...] = jnp.zeros_like(dg_ref)

            dg_ref[...] += jnp.sum(dyv * xh, axis=0, keepdims=True)

    row = pl.BlockSpec((tr, D), lambda i: (i, 0))
    vec = pl.BlockSpec((1, D), lambda i: (0, 0))
    if has_gain:
        dx, dg = pl.pallas_call(
            body, out_shape=(jax.ShapeDtypeStruct((R, D), F32), jax.ShapeDtypeStruct((1, D), F32)),
            grid=(R // tr,), in_specs=[row, vec, row], out_specs=(row, vec), name="norm_bwd",
            compiler_params=_params(("arbitrary",)))(x, gain.reshape(1, D), dy)
        return dx, dg.reshape(D)
    dx = pl.pallas_call(body, out_shape=jax.ShapeDtypeStruct((R, D), F32), grid=(R // tr,),
                        in_specs=[row, row], out_specs=row, name="l2norm_bwd",
                        compiler_params=_params(("parallel",)))(x, dy)
    return dx, None


@functools.partial(jax.custom_vjp, nondiff_argnums=(2,))
def rms_norm(x, gain, eps):
    return _norm_fwd_call(x, gain, eps, 1.0 / x.shape[-1])


def _rms_norm_fwd(x, gain, eps):
    return rms_norm(x, gain, eps), (x, gain)


def _rms_norm_bwd(eps, saved, dy):
    x, gain = saved
    return _norm_bwd_call(x, gain, dy, eps, 1.0 / x.shape[-1])


rms_norm.defvjp(_rms_norm_fwd, _rms_norm_bwd)


@jax.custom_vjp
def l2_norm(x):
    return _norm_fwd_call(x, None, EPS, 1.0)


def _l2_norm_fwd(x):
    return l2_norm(x), x


def _l2_norm_bwd(x, dy):
    return (_norm_bwd_call(x, None, dy, EPS, 1.0)[0],)


l2_norm.defvjp(_l2_norm_fwd, _l2_norm_bwd)


def _attn_mask(tq, sk):
    q0 = pl.program_id(1) * tq
    qc = (q0 + lax.broadcasted_iota(jnp.int32, (tq, sk), 0)) // CHUNK
    kc = lax.broadcasted_iota(jnp.int32, (tq, sk), 1) // CHUNK
    return kc <= qc


def _attn_fwd_call(q, k, v, scale, causal):
    H, S, dk = q.shape
    Sk, dv = k.shape[1], v.shape[2]
    tq = _pick(S, (256, 128, 64))

    def body(q_ref, k_ref, v_ref, o_ref, lse_ref):
        s = lax.dot_general(q_ref[...].astype(BF16), k_ref[...].astype(BF16), NT,
                            preferred_element_type=F32) * scale
        if causal:
            s = jnp.where(_attn_mask(tq, Sk), s, -1e30)
        m = jnp.max(s, axis=-1, keepdims=True)
        p = jnp.exp(s - m)
        l = jnp.sum(p, axis=-1, keepdims=True)
        o = lax.dot_general(p.astype(BF16), v_ref[...].astype(BF16), NN, preferred_element_type=F32)
        o_ref[...] = o / l
        lse_ref[...] = m + jnp.log(l)

    return pl.pallas_call(
        body,
        out_shape=(jax.ShapeDtypeStruct((H, S, dv), F32), jax.ShapeDtypeStruct((H, S, 1), F32)),
        grid=(H, S // tq),
        in_specs=[pl.BlockSpec((None, tq, dk), lambda h, i: (h, i, 0)),
                  pl.BlockSpec((None, Sk, dk), lambda h, i: (h, 0, 0)),
                  pl.BlockSpec((None, Sk, dv), lambda h, i: (h, 0, 0))],
        out_specs=(pl.BlockSpec((None, tq, dv), lambda h, i: (h, i, 0)),
                   pl.BlockSpec((None, tq, 1), lambda h, i: (h, i, 0))),
        name="attn_fwd", compiler_params=_params(("parallel", "parallel")))(q, k, v)


def _attn_bwd_call(q, k, v, o, lse, do, scale, causal):
    H, S, dk = q.shape
    Sk, dv = k.shape[1], v.shape[2]
    tq = _pick(S, (256, 128, 64))

    def body(q_ref, k_ref, v_ref, o_ref, lse_ref, do_ref, dq_ref, dk_ref, dv_ref):
        qb = q_ref[...].astype(BF16)
        kb = k_ref[...].astype(BF16)
        dob = do_ref[...].astype(BF16)
        s = lax.dot_general(qb, kb, NT, preferred_element_type=F32) * scale
        if causal:
            s = jnp.where(_attn_mask(tq, Sk), s, -1e30)
        p = jnp.exp(s - lse_ref[...])
        dp = lax.dot_general(dob, v_ref[...].astype(BF16), NT, preferred_element_type=F32)
        delta = jnp.sum(do_ref[...] * o_ref[...], axis=-1, keepdims=True)
        ds = (p * (dp - delta) * scale).astype(BF16)
        dq_ref[...] = lax.dot_general(ds, kb, NN, preferred_element_type=F32)

        @pl.when(pl.program_id(1) == 0)
        def _():
            dk_ref[...] = jnp.zeros_like(dk_ref)
            dv_ref[...] = jnp.zeros_like(dv_ref)

        dk_ref[...] += lax.dot_general(ds, qb, TN, preferred_element_type=F32)
        dv_ref[...] += lax.dot_general(p.astype(BF16), dob, TN, preferred_element_type=F32)

    qs = pl.BlockSpec((None, tq, dk), lambda h, i: (h, i, 0))
    ks = pl.BlockSpec((None, Sk, dk), lambda h, i: (h, 0, 0))
    vs = pl.BlockSpec((None, Sk, dv), lambda h, i: (h, 0, 0))
    os_ = pl.BlockSpec((None, tq, dv), lambda h, i: (h, i, 0))
    ls = pl.BlockSpec((None, tq, 1), lambda h, i: (h, i, 0))
    return pl.pallas_call(
        body,
        out_shape=(jax.ShapeDtypeStruct(q.shape, F32), jax.ShapeDtypeStruct(k.shape, F32),
                   jax.ShapeDtypeStruct(v.shape, F32)),
        grid=(H, S // tq), in_specs=[qs, ks, vs, os_, ls, os_], out_specs=(qs, ks, vs),
        name="attn_bwd", compiler_params=_params(("parallel", "arbitrary")))(q, k, v, o, lse, do)


@functools.partial(jax.custom_vjp, nondiff_argnums=(3, 4))
def attention(q, k, v, scale, causal):
    return _attn_fwd_call(q, k, v, scale, causal)[0]


def _attention_fwd(q, k, v, scale, causal):
    o, lse = _attn_fwd_call(q, k, v, scale, causal)
    return o, (q, k, v, o, lse)


def _attention_bwd(scale, causal, saved, do):
    q, k, v, o, lse = saved
    return _attn_bwd_call(q, k, v, o, lse, do, scale, causal)


attention.defvjp(_attention_fwd, _attention_bwd)


NNB = (((2,), (1,)), ((0,), (0,)))
NTB = (((2,), (2,)), ((0,), (0,)))
TNB = (((1,), (1,)), ((0,), (0,)))


def _dot3(a, b, dims):
    a_hi, b_hi = a.astype(BF16), b.astype(BF16)
    a_lo = (a - a_hi.astype(F32)).astype(BF16)
    b_lo = (b - b_hi.astype(F32)).astype(BF16)

    def dot(x, y):
        return lax.dot_general(x, y, dims, preferred_element_type=F32)

    return dot(a_hi, b_hi) + (dot(a_hi, b_lo) + dot(a_lo, b_hi))


@functools.partial(jax.custom_vjp, nondiff_argnums=(2,))
def _hdot(a, b, dims):
    return _dot3(a, b, dims)


def _hdot_fwd(a, b, dims):
    return _dot3(a, b, dims), (a, b)


def _hdot_bwd(dims, saved, g):
    a, b = saved
    if dims == NNB:
        return _dot3(g, b, NTB), _dot3(a, g, TNB)
    if dims == NTB:
        return _dot3(g, b, NNB), _dot3(g, a, TNB)
    assert dims == TNB
    return _dot3(b, g, NTB), _dot3(a, g, NNB)


_hdot.defvjp(_hdot_fwd, _hdot_bwd)


def _dn_chunk(q, k, v, gcol, grow, bcol, state):
    c = q.shape[1]
    r_i = lax.broadcasted_iota(jnp.int32, (1, c, c), 1)
    c_i = lax.broadcasted_iota(jnp.int32, (1, c, c), 2)
    incl = c_i <= r_i
    strict = c_i < r_i
    g_cum_col = jnp.sum(jnp.where(incl, grow, 0.0), axis=2, keepdims=True)
    g_cum_row = jnp.sum(jnp.where(r_i <= c_i, gcol, 0.0), axis=1, keepdims=True)
    g_last = jnp.sum(grow, axis=2, keepdims=True)
    decay = jnp.where(incl, jnp.exp(jnp.where(incl, g_cum_col - g_cum_row, 0.0)), 0.0)
    qs = q * (q.shape[2] ** -0.5)
    kb = k * bcol
    x = -jnp.where(strict, _hdot(kb, k, NTB) * decay, 0.0)
    eye = (r_i == c_i).astype(F32)
    t = eye + x
    steps = max(1, int(np.ceil(np.log2(c))) - 1)
    for _ in range(steps):
        x = _hdot(x, x, NNB)
        t = t + _hdot(t, x, NNB)
    e_col = jnp.exp(g_cum_col)
    u = _hdot(t, v * bcol, NNB)
    w = _hdot(t, kb * e_col, NNB)
    attn = _hdot(qs, k, NTB) * decay
    v_new = u - _hdot(w, state, NNB)
    o = _hdot(qs * e_col, state, NNB) + _hdot(attn, v_new, NNB)
    k_dec = k * jnp.exp(g_last - g_cum_col)
    new_state = state * jnp.exp(g_last) + _hdot(k_dec, v_new, TNB)
    return o, new_state


DN_HEADS_PER_STEP = 8


def _dn_specs(hb, D, chunk_index):
    seq = pl.BlockSpec((hb, CHUNK, D), lambda h, n: (h, chunk_index(n), 0))
    col = pl.BlockSpec((hb, CHUNK, 1), lambda h, n: (h, chunk_index(n), 0))
    row = pl.BlockSpec((hb, None, 1, CHUNK), lambda h, n: (h, chunk_index(n), 0, 0))
    st = pl.BlockSpec((hb, None, D, D), lambda h, n: (h, chunk_index(n), 0, 0))
    return seq, col, row, st


def _dn_fwd_call(q, k, v, gcol, grow, bcol):
    H, S, D = q.shape
    N = S // CHUNK
    hb = _pick(H, (DN_HEADS_PER_STEP, 2, 1))
    seq, col, row, st = _dn_specs(hb, D, lambda n: n)

    def body(q_ref, k_ref, v_ref, gc_ref, gr_ref, bc_ref, o_ref, st_ref, state):
        @pl.when(pl.program_id(1) == 0)
        def _():
            state[...] = jnp.zeros_like(state)

        s_in = state[...]
        st_ref[...] = s_in
        o, s_out = _dn_chunk(q_ref[...], k_ref[...], v_ref[...], gc_ref[...], gr_ref[...], bc_ref[...], s_in)
        o_ref[...] = o
        state[...] = s_out

    return pl.pallas_call(
        body,
        out_shape=(jax.ShapeDtypeStruct((H, S, D), F32), jax.ShapeDtypeStruct((H, N, D, D), F32)),
        grid=(H // hb, N), in_specs=[seq, seq, seq, col, row, col], out_specs=(seq, st),
        scratch_shapes=[pltpu.VMEM((hb, D, D), F32)], name="deltanet_fwd",
        compiler_params=_params(("parallel", "arbitrary")))(q, k, v, gcol, grow, bcol)


def _dn_bwd_call(q, k, v, gcol, grow, bcol, states, do):
    H, S, D = q.shape
    N = S // CHUNK
    hb = _pick(H, (DN_HEADS_PER_STEP, 2, 1))
    seq, col, row, st = _dn_specs(hb, D, lambda n: N - 1 - n)

    def body(q_ref, k_ref, v_ref, gc_ref, gr_ref, bc_ref, st_ref, do_ref,
             dq_ref, dk_ref, dv_ref, dgc_ref, dgr_ref, dbc_ref, dstate):
        @pl.when(pl.program_id(1) == 0)
        def _():
            dstate[...] = jnp.zeros_like(dstate)

        _, vjp = jax.vjp(_dn_chunk, q_ref[...], k_ref[...], v_ref[...], gc_ref[...], gr_ref[...], bc_ref[...],
                         st_ref[...])
        grads = vjp((do_ref[...], dstate[...]))
        for ref, val in zip((dq_ref, dk_ref, dv_ref, dgc_ref, dgr_ref, dbc_ref, dstate), grads):
            ref[...] = val

    sd = jax.ShapeDtypeStruct
    return pl.pallas_call(
        body,
        out_shape=(sd((H, S, D), F32), sd((H, S, D), F32), sd((H, S, D), F32),
                   sd((H, S, 1), F32), sd((H, N, 1, CHUNK), F32), sd((H, S, 1), F32)),
        grid=(H // hb, N), in_specs=[seq, seq, seq, col, row, col, st, seq],
        out_specs=(seq, seq, seq, col, row, col),
        scratch_shapes=[pltpu.VMEM((hb, D, D), F32)], name="deltanet_bwd",
        compiler_params=_params(("parallel", "arbitrary")))(q, k, v, gcol, grow, bcol, states, do)


@jax.custom_vjp
def delta_rule(q, k, v, gcol, grow, bcol):
    return _dn_fwd_call(q, k, v, gcol, grow, bcol)[0]


def _delta_rule_fwd(q, k, v, gcol, grow, bcol):
    o, states = _dn_fwd_call(q, k, v, gcol, grow, bcol)
    return o, (q, k, v, gcol, grow, bcol, states)


def _delta_rule_bwd(saved, do):
    return _dn_bwd_call(*saved, do)


delta_rule.defvjp(_delta_rule_fwd, _delta_rule_bwd)


def _shift_down(x, j):
    if j == 0:
        return x
    rows = lax.broadcasted_iota(jnp.int32, x.shape, 0)
    return jnp.where(rows >= j, pltpu.roll(x, j, 0), 0.0)


def _shift_up(x, j):
    if j == 0:
        return x
    s = x.shape[0]
    rows = lax.broadcasted_iota(jnp.int32, x.shape, 0)
    return jnp.where(rows < s - j, pltpu.roll(x, s - j, 0), 0.0)


def _conv(x, w):
    kw = w.shape[0]
    acc = x * w[kw - 1:kw, :]
    for kk in range(kw - 1):
        acc = acc + _shift_down(x, kw - 1 - kk) * w[kk:kk + 1, :]
    return acc


def _conv_t(d, w):
    kw = w.shape[0]
    acc = d * w[kw - 1:kw, :]
    for kk in range(kw - 1):
        acc = acc + _shift_up(d, kw - 1 - kk) * w[kk:kk + 1, :]
    return acc


def _conv_dw_rows(d, x, kw):
    return [jnp.sum(d * _shift_down(x, kw - 1 - kk), axis=0, keepdims=True) for kk in range(kw)]


def _silu(a):
    return a * jax.nn.sigmoid(a)


def _dsilu(a):
    s = jax.nn.sigmoid(a)
    return s * (1.0 + a * (1.0 - s))


def _conv_silu_fwd_call(x, w):
    S, C = x.shape
    kw = w.shape[0]
    tc = _pick(C, (256, 128))

    def body(x_ref, w_ref, y_ref):
        y_ref[...] = _silu(_conv(x_ref[...], w_ref[...]))

    xs = pl.BlockSpec((S, tc), lambda j: (0, j))
    ws = pl.BlockSpec((kw, tc), lambda j: (0, j))
    return pl.pallas_call(body, out_shape=jax.ShapeDtypeStruct((S, C), F32), grid=(C // tc,),
                          in_specs=[xs, ws], out_specs=xs, name="conv_silu_fwd",
                          compiler_params=_params(("parallel",)))(x, w)


def _conv_silu_bwd_call(x, w, dy):
    S, C = x.shape
    kw = w.shape[0]
    tc = _pick(C, (256, 128))

    def body(x_ref, w_ref, dy_ref, dx_ref, dw_ref):
        xv = x_ref[...]
        wv = w_ref[...]
        da = dy_ref[...] * _dsilu(_conv(xv, wv))
        dx_ref[...] = _conv_t(da, wv)
        for kk, row in enumerate(_conv_dw_rows(da, xv, kw)):
            dw_ref[kk:kk + 1, :] = row

    xs = pl.BlockSpec((S, tc), lambda j: (0, j))
    ws = pl.BlockSpec((kw, tc), lambda j: (0, j))
    return pl.pallas_call(
        body, out_shape=(jax.ShapeDtypeStruct((S, C), F32), jax.ShapeDtypeStruct((kw, C), F32)),
        grid=(C // tc,), in_specs=[xs, ws, xs], out_specs=(xs, ws), name="conv_silu_bwd",
        compiler_params=_params(("parallel",)))(x, w, dy)


@jax.custom_vjp
def conv_silu(x, w):
    return _conv_silu_fwd_call(x, w)


def _conv_silu_fwd(x, w):
    return conv_silu(x, w), (x, w)


def _conv_silu_bwd(saved, dy):
    return _conv_silu_bwd_call(*saved, dy)


conv_silu.defvjp(_conv_silu_fwd, _conv_silu_bwd)


def _conv_glu_fwd_call(x, w, b, tc):
    S, C2 = x.shape
    kw = w.shape[0]
    nb = C2 // (2 * tc)

    def body(x_ref, w_ref, b_ref, y_ref):
        a = _conv(x_ref[...], w_ref[...]) + b_ref[...]
        y_ref[...] = _silu(a[:, :tc]) * a[:, tc:]

    return pl.pallas_call(
        body, out_shape=jax.ShapeDtypeStruct((S, C2 // 2), F32), grid=(nb,),
        in_specs=[pl.BlockSpec((S, 2 * tc), lambda j: (0, j)), pl.BlockSpec((kw, 2 * tc), lambda j: (0, j)),
                  pl.BlockSpec((1, 2 * tc), lambda j: (0, j))],
        out_specs=pl.BlockSpec((S, tc), lambda j: (0, j)), name="conv_glu_fwd",
        compiler_params=_params(("parallel",)))(x, w, b)


def _conv_glu_bwd_call(x, w, b, dy, tc):
    S, C2 = x.shape
    kw = w.shape[0]
    nb = C2 // (2 * tc)

    def body(x_ref, w_ref, b_ref, dy_ref, dx_ref, dw_ref, db_ref):
        xv = x_ref[...]
        wv = w_ref[...]
        a = _conv(xv, wv) + b_ref[...]
        ag, au = a[:, :tc], a[:, tc:]
        dyv = dy_ref[...]
        da = jnp.concatenate([dyv * au * _dsilu(ag), dyv * _silu(ag)], axis=1)
        dx_ref[...] = _conv_t(da, wv)
        for kk, row in enumerate(_conv_dw_rows(da, xv, kw)):
            dw_ref[kk:kk + 1, :] = row
        db_ref[...] = jnp.sum(da, axis=0, keepdims=True)

    xs = pl.BlockSpec((S, 2 * tc), lambda j: (0, j))
    ws = pl.BlockSpec((kw, 2 * tc), lambda j: (0, j))
    bs = pl.BlockSpec((1, 2 * tc), lambda j: (0, j))
    return pl.pallas_call(
        body, out_shape=(jax.ShapeDtypeStruct((S, C2), F32), jax.ShapeDtypeStruct((kw, C2), F32),
                         jax.ShapeDtypeStruct((1, C2), F32)),
        grid=(nb,), in_specs=[xs, ws, bs, pl.BlockSpec((S, tc), lambda j: (0, j))], out_specs=(xs, ws, bs),
        name="conv_glu_bwd", compiler_params=_params(("parallel",)))(x, w, b, dy)


@functools.partial(jax.custom_vjp, nondiff_argnums=(3,))
def conv_glu(x, w, b, tc):
    return _conv_glu_fwd_call(x, w, b, tc)


def _conv_glu_fwd(x, w, b, tc):
    return conv_glu(x, w, b, tc), (x, w, b)


def _conv_glu_bwd(tc, saved, dy):
    return _conv_glu_bwd_call(*saved, dy, tc)


conv_glu.defvjp(_conv_glu_fwd, _conv_glu_bwd)


def _sqerr_call(y, t):
    S, D = y.shape
    tr = _row_tile(S, D)

    def body(y_ref, t_ref, l_ref, dy_ref):
        d = y_ref[...] - t_ref[...]
        dy_ref[...] = d * (1.0 / D)

        @pl.when(pl.program_id(0) == 0)
        def _():
            l_ref[...] = jnp.zeros_like(l_ref)

        l_ref[...] += jnp.full((8, LANES), 0.5 / D, F32) * jnp.sum(d * d)

    row = pl.BlockSpec((tr, D), lambda i: (i, 0))
    one = pl.BlockSpec((8, LANES), lambda i: (0, 0))
    return pl.pallas_call(
        body, out_shape=(jax.ShapeDtypeStruct((8, LANES), F32), jax.ShapeDtypeStruct((S, D), F32)),
        grid=(S // tr,), in_specs=[row, row], out_specs=(one, row), name="loss_head",
        compiler_params=_params(("arbitrary",)))(y, t)


@jax.custom_vjp
def sq_loss(y, t):
    return _sqerr_call(y, t)[0][0, 0]


def _sq_loss_fwd(y, t):
    l, dy = _sqerr_call(y, t)
    return l[0, 0], dy


def _sq_loss_bwd(dy, dl):
    return dy * dl, None


sq_loss.defvjp(_sq_loss_fwd, _sq_loss_bwd)


def adamw(w, g, m, v):
    shape = w.shape
    C = shape[-1]
    R = int(np.prod(shape[:-1]))
    tr = _pick(R, tuple(t for t in (4096, 2048, 1024, 512, 256, 128, 64, 32, 16, 8) if t * C * 4 <= (1 << 20)))
    c1 = 1.0 / (1.0 - ADAM_B1 ** ADAM_STEP)
    c2 = 1.0 / (1.0 - ADAM_B2 ** ADAM_STEP)

    def body(w_ref, g_ref, m_ref, v_ref, d_ref, nm_ref, nv_ref):
        gv = g_ref[...]
        nm = ADAM_B1 * m_ref[...] + (1.0 - ADAM_B1) * gv
        nv = ADAM_B2 * v_ref[...] + (1.0 - ADAM_B2) * (gv * gv)
        d_ref[...] = -ADAM_LR * ((nm * c1) / (jnp.sqrt(nv * c2) + ADAM_EPS) + ADAM_WD * w_ref[...])
        nm_ref[...] = nm
        nv_ref[...] = nv

    blk = pl.BlockSpec((tr, C), lambda i: (i, 0))
    out = jax.ShapeDtypeStruct((R, C), F32)
    d, nm, nv = pl.pallas_call(body, out_shape=(out, out, out), grid=(R // tr,), in_specs=[blk] * 4,
                               out_specs=(blk, blk, blk), name="adamw",
                               compiler_params=_params(("parallel",)))(
        w.reshape(R, C), g.reshape(R, C), m.reshape(R, C), v.reshape(R, C))
    return d.reshape(shape), nm.reshape(shape), nv.reshape(shape)


def _place():
    x, y, c = lax.axis_index("x"), lax.axis_index("y"), lax.axis_index("c")
    chips = [(1 - x, y), (x, 1 - y), (1 - x, 1 - y)]
    return x, y, c, chips


def small_allgather(buf):
    m_per, n = buf.shape

    def body(x_ref, out_ref, send_sems, recv_sems, local_sem):
        x, y, c, chips = _place()
        me, sibling = (x, y, c), (x, y, 1 - c)

        def rows(px, py, pc):
            return out_ref.at[4 * px + 2 * py + pc]

        def copy(k, block, to, src=None):
            return pltpu.make_async_remote_copy(
                src_ref=rows(*block) if src is None else src, dst_ref=rows(*block),
                send_sem=send_sems.at[k], recv_sem=recv_sems.at[k], device_id=to, device_id_type=MESH)

        mine = pltpu.make_async_copy(x_ref, rows(*me), local_sem)
        mine.start()
        first = [copy(0, me, sibling, src=x_ref)]
        first += [copy(1 + j, me, (*chip, c), src=x_ref) for j, chip in enumerate(chips)]
        for cp in first:
            cp.start()
        passed = [copy(4 + j, (*chip, c), sibling) for j, chip in enumerate(chips)]
        for j, chip in enumerate(chips):
            copy(1 + j, (*chip, c), me).wait_recv()
            passed[j].start()
        copy(0, sibling, me).wait_recv()
        for j, chip in enumerate(chips):
            copy(4 + j, (*chip, 1 - c), me).wait_recv()
        for cp in first + passed:
            cp.wait_send()
        mine.wait()

    return pl.pallas_call(
        body, out_shape=jax.ShapeDtypeStruct((N_DEV, m_per, n), buf.dtype),
        in_specs=[pl.BlockSpec(memory_space=pltpu.VMEM)], out_specs=pl.BlockSpec(memory_space=pltpu.VMEM),
        scratch_shapes=[pltpu.SemaphoreType.DMA((7,)), pltpu.SemaphoreType.DMA((7,)), pltpu.SemaphoreType.DMA],
        name="small_allgather", compiler_params=pltpu.CompilerParams(vmem_limit_bytes=VMEM_LIMIT))(buf)


def sum_blocks(g):
    n, R, C = g.shape
    tr = _pick(R, (512, 256, 128, 64, 32, 16, 8))

    def body(g_ref, o_ref):
        acc = g_ref[0].astype(F32)
        for d in range(1, n):
            acc = acc + g_ref[d].astype(F32)
        o_ref[...] = acc

    return pl.pallas_call(body, out_shape=jax.ShapeDtypeStruct((R, C), F32), grid=(R // tr,),
                          in_specs=[pl.BlockSpec((n, tr, C), lambda i: (0, i, 0))],
                          out_specs=pl.BlockSpec((tr, C), lambda i: (i, 0)), name="sum_blocks",
                          compiler_params=_params(("parallel",)))(g)


def _hbm_specs(n):
    return [pl.BlockSpec(memory_space=pl.ANY)] * n


def gather_weights(shards):
    n = len(shards)

    def body(*refs):
        src, out = refs[:n], refs[n:2 * n]
        send_sems, recv_sems = refs[2 * n:]
        x, y, c, chips = _place()
        sibling = (x, y, 1 - c)
        my_chip = 2 * x + y
        sends, passes = [], []
        for t in range(n):
            half = src[t].shape[0] // 2
            for j, chip in enumerate(chips):
                cp = pltpu.make_async_remote_copy(
                    src_ref=src[t].at[pl.ds(c * half, half)], dst_ref=out[t].at[my_chip, pl.ds(c * half, half)],
                    send_sem=send_sems.at[t, j], recv_sem=recv_sems.at[t, j],
                    device_id=(*chip, c), device_id_type=MESH)
                cp.start()
                sends.append(cp)
        for t in range(n):
            half = src[t].shape[0] // 2
            for j, chip in enumerate(chips):
                landed = out[t].at[2 * chip[0] + chip[1], pl.ds(c * half, half)]
                pltpu.make_async_remote_copy(
                    src_ref=landed, dst_ref=landed, send_sem=send_sems.at[t, j], recv_sem=recv_sems.at[t, j],
                    device_id=(*chip, c), device_id_type=MESH).wait_recv()
                fw = pltpu.make_async_remote_copy(
                    src_ref=landed, dst_ref=landed, send_sem=send_sems.at[t, 3 + j], recv_sem=recv_sems.at[t, 3 + j],
                    device_id=sibling, device_id_type=MESH)
                fw.start()
                passes.append(fw)
        for t in range(n):
            half = src[t].shape[0] // 2
            for j, chip in enumerate(chips):
                other = out[t].at[2 * chip[0] + chip[1], pl.ds((1 - c) * half, half)]
                pltpu.make_async_remote_copy(
                    src_ref=other, dst_ref=other, send_sem=send_sems.at[t, 3 + j], recv_sem=recv_sems.at[t, 3 + j],
                    device_id=sibling, device_id_type=MESH).wait_recv()
        for cp in sends + passes:
            cp.wait_send()

    out_shape = tuple(jax.ShapeDtypeStruct((N_CHIPS,) + s.shape, s.dtype) for s in shards)
    return pl.pallas_call(
        body, out_shape=out_shape, in_specs=_hbm_specs(n), out_specs=tuple(_hbm_specs(n)),
        scratch_shapes=[pltpu.SemaphoreType.DMA((n, 6)), pltpu.SemaphoreType.DMA((n, 6))],
        name="gather_weights", compiler_params=pltpu.CompilerParams(has_side_effects=True))(*shards)


def sibling_exchange_halves(grads):
    n = len(grads)

    def body(*refs):
        src, out = refs[:n], refs[n:2 * n]
        send_sems, recv_sems = refs[2 * n:]
        x, y, c, _ = _place()
        sibling = (x, y, 1 - c)
        cps = []
        for t in range(n):
            half = src[t].shape[1] // 2
            cp = pltpu.make_async_remote_copy(
                src_ref=src[t].at[:, pl.ds((1 - c) * half, half)], dst_ref=out[t],
                send_sem=send_sems.at[t], recv_sem=recv_sems.at[t], device_id=sibling, device_id_type=MESH)
            cp.start()
            cps.append(cp)
        for cp in cps:
            cp.wait()

    out_shape = tuple(jax.ShapeDtypeStruct((s.shape[0], s.shape[1] // 2, s.shape[2]), s.dtype) for s in grads)
    return pl.pallas_call(
        body, out_shape=out_shape, in_specs=_hbm_specs(n), out_specs=tuple(_hbm_specs(n)),
        scratch_shapes=[pltpu.SemaphoreType.DMA((n,)), pltpu.SemaphoreType.DMA((n,))],
        name="sibling_exchange_halves", compiler_params=pltpu.CompilerParams(has_side_effects=True))(*grads)


def pair_sum(g, recv, c_idx):
    G, R, C = g.shape
    half = R // 2
    tr = _pick(half, tuple(t for t in (1024, 512, 256, 128, 64, 32, 16) if t * C * 2 <= (1 << 20)))
    g4 = g.reshape(G, 2, half, C)

    def body(c_ref, g_ref, r_ref, o_ref):
        o_ref[...] = (g_ref[...].astype(F32) + r_ref[...].astype(F32)).astype(o_ref.dtype)

    grid_spec = pltpu.PrefetchScalarGridSpec(
        num_scalar_prefetch=1, grid=(G, half // tr),
        in_specs=[pl.BlockSpec((None, None, tr, C), lambda i, j, c_ref: (i, c_ref[0], j, 0)),
                  pl.BlockSpec((None, tr, C), lambda i, j, c_ref: (i, j, 0))],
        out_specs=pl.BlockSpec((None, tr, C), lambda i, j, c_ref: (i, j, 0)))
    return pl.pallas_call(body, out_shape=jax.ShapeDtypeStruct((G, half, C), g.dtype), grid_spec=grid_spec,
                          name="pair_sum", compiler_params=_params(("parallel", "parallel")))(c_idx, g4, recv)


def chip_exchange(parts):
    n = len(parts)

    def body(*refs):
        src, out = refs[:n], refs[n:2 * n]
        send_sems, recv_sems = refs[2 * n:]
        x, y, c, chips = _place()
        cps = []
        for t in range(n):
            for j, chip in enumerate(chips):
                cp = pltpu.make_async_remote_copy(
                    src_ref=src[t].at[2 * chip[0] + chip[1]], dst_ref=out[t].at[j],
                    send_sem=send_sems.at[t, j], recv_sem=recv_sems.at[t, j],
                    device_id=(*chip, c), device_id_type=MESH)
                cp.start()
                cps.append(cp)
        for cp in cps:
            cp.wait()

    out_shape = tuple(jax.ShapeDtypeStruct((3,) + s.shape[1:], s.dtype) for s in parts)
    return pl.pallas_call(
        body, out_shape=out_shape, in_specs=_hbm_specs(n), out_specs=tuple(_hbm_specs(n)),
        scratch_shapes=[pltpu.SemaphoreType.DMA((n, 3)), pltpu.SemaphoreType.DMA((n, 3))],
        name="chip_exchange", compiler_params=pltpu.CompilerParams(has_side_effects=True))(*parts)


def sum_slots_into(parts, recv, chip_idx, acc, layer, depth):
    _, H, C = parts.shape
    tr = _pick(H, tuple(t for t in (1024, 512, 256, 128, 64, 32, 16) if t * C * 4 <= (1 << 20)))
    has_acc = acc is not None

    def body(chip_ref, p_ref, r_ref, *rest):
        o_ref = rest[-1]
        o_ref[...] = ((r_ref[0].astype(F32) + r_ref[1].astype(F32)) + r_ref[2].astype(F32)) + p_ref[...].astype(F32)

    in_specs = [pl.BlockSpec((None, tr, C), lambda i, chip_ref: (chip_ref[0], i, 0)),
                pl.BlockSpec((3, tr, C), lambda i, chip_ref: (0, i, 0))]
    args = [chip_idx, parts, recv]
    if has_acc:
        in_specs.append(pl.BlockSpec(memory_space=pl.ANY))
        args.append(acc)
    grid_spec = pltpu.PrefetchScalarGridSpec(
        num_scalar_prefetch=1, grid=(H // tr,), in_specs=in_specs,
        out_specs=pl.BlockSpec((None, tr, C), lambda i, chip_ref: (layer, i, 0)))
    return pl.pallas_call(body, out_shape=jax.ShapeDtypeStruct((depth, H, C), F32), grid_spec=grid_spec,
                          input_output_aliases=({3: 0} if has_acc else {}), name="sum_slots",
                          compiler_params=_params(("parallel",)))(*args)


def sibling_swap(mine):
    n = len(mine)

    def body(*refs):
        src, out = refs[:n], refs[n:2 * n]
        send_sems, recv_sems = refs[2 * n:]
        x, y, c, _ = _place()
        cps = []
        for t in range(n):
            cp = pltpu.make_async_remote_copy(
                src_ref=src[t], dst_ref=out[t], send_sem=send_sems.at[t], recv_sem=recv_sems.at[t],
                device_id=(x, y, 1 - c), device_id_type=MESH)
            cp.start()
            cps.append(cp)
        for cp in cps:
            cp.wait()

    out_shape = tuple(jax.ShapeDtypeStruct(s.shape, s.dtype) for s in mine)
    return pl.pallas_call(
        body, out_shape=out_shape, in_specs=_hbm_specs(n), out_specs=tuple(_hbm_specs(n)),
        scratch_shapes=[pltpu.SemaphoreType.DMA((n,)), pltpu.SemaphoreType.DMA((n,))],
        name="sibling_swap", compiler_params=pltpu.CompilerParams(has_side_effects=True))(*mine)


def reduce_layer(grads, c_idx, chip_idx, accs, layer, depth):
    recv = sibling_exchange_halves(grads)
    parts = [pair_sum(g, r, c_idx) for g, r in zip(grads, recv)]
    slots = chip_exchange(parts)
    return [sum_slots_into(p, s, chip_idx, a, layer, depth) for p, s, a in zip(parts, slots, accs)]


def adamw_halves(w, mine, theirs, m, v, c_idx):
    L, R, C = w.shape
    half = R // 2
    tr = _pick(half, tuple(t for t in (2048, 1024, 512, 256, 128, 64, 32, 16, 8) if t * C * 4 <= (1 << 20)))
    nb = half // tr
    c1 = 1.0 / (1.0 - ADAM_B1 ** ADAM_STEP)
    c2 = 1.0 / (1.0 - ADAM_B2 ** ADAM_STEP)

    def body(c_ref, w_ref, a_ref, b_ref, m_ref, v_ref, g_ref, d_ref, nm_ref, nv_ref):
        gv = jnp.where(c_ref[0] == pl.program_id(1), a_ref[...], b_ref[...])
        nm = ADAM_B1 * m_ref[...] + (1.0 - ADAM_B1) * gv
        nv = ADAM_B2 * v_ref[...] + (1.0 - ADAM_B2) * (gv * gv)
        g_ref[...] = gv
        d_ref[...] = -ADAM_LR * ((nm * c1) / (jnp.sqrt(nv * c2) + ADAM_EPS) + ADAM_WD * w_ref[...])
        nm_ref[...] = nm
        nv_ref[...] = nv

    full = pl.BlockSpec((None, tr, C), lambda l, h, i, c_ref: (l, h * nb + i, 0))
    part = pl.BlockSpec((None, tr, C), lambda l, h, i, c_ref: (l, i, 0))
    grid_spec = pltpu.PrefetchScalarGridSpec(
        num_scalar_prefetch=1, grid=(L, 2, nb), in_specs=[full, part, part, full, full],
        out_specs=(full, full, full, full))
    out = jax.ShapeDtypeStruct((L, R, C), F32)
    return pl.pallas_call(body, out_shape=(out, out, out, out), grid_spec=grid_spec, name="adamw_halves",
                          compiler_params=_params(("parallel", "parallel", "parallel")))(c_idx, w, mine, theirs, m, v)


BIG = ("w_in", "mla_w_qb", "mla_w_kvb", "w_out", "xa_wq", "xa_wk", "xa_wv", "xa_wo", "ffn_w_up", "ffn_w_down")
COL_SHARDED = ("w_in", "mla_w_qb", "mla_w_kvb", "ffn_w_up")
SMALL_REPL = ("norm_mix", "dn_a_log", "dn_dt_bias", "dn_out_norm", "mla_q_norm", "mla_kv_norm", "mem_norm",
              "norm_xattn", "norm_ffn", "ffn_conv_bias", "norm_final")
SMALL_SHARDED = ("dn_conv", "ffn_conv")
WEIGHTS = ("norm_mix", "w_in", "dn_conv", "dn_a_log", "dn_dt_bias", "dn_out_norm", "mla_q_norm", "mla_w_qb",
           "mla_kv_norm", "mla_w_kvb", "w_out", "mem_norm", "norm_xattn", "xa_wq", "xa_wk", "xa_wv", "xa_wo",
           "norm_ffn", "ffn_w_up", "ffn_conv", "ffn_conv_bias", "ffn_w_down", "norm_final")


def _pad_cols(a, cp):
    c = a.shape[-1]
    if c == cp:
        return a
    return jnp.pad(a, [(0, 0)] * (a.ndim - 1) + [(0, cp - c)])


def _pack(arrs):
    flat = jnp.concatenate([a.reshape(-1).astype(F32) for a in arrs])
    rows = -(-flat.shape[0] // LANES)
    rows = -(-rows // 8) * 8
    return jnp.pad(flat, (0, rows * LANES - flat.shape[0])).reshape(rows, LANES)


def _unpack(buf, like):
    flat = buf.reshape(-1)
    out, off = [], 0
    for a in like:
        n = int(np.prod(a.shape))
        out.append(flat[off:off + n].reshape(a.shape))
        off += n
    return out


def _heads(t, h):
    s = t.shape[0]
    return jnp.transpose(t.reshape(s, h, t.shape[1] // h), (1, 0, 2))


def _unheads(t):
    h, s, d = t.shape
    return jnp.transpose(t, (1, 0, 2)).reshape(s, h * d)


def _rope(t, cos, sin):
    t1, t2 = jnp.split(t, 2, axis=-1)
    return jnp.concatenate([t1 * cos - t2 * sin, t2 * cos + t1 * sin], axis=-1)


def _forward_loss(p, x, mem, cos, sin, target, dims):
    S, D = x.shape
    depth, dn_h, mla_h, d_ff, c_in, c_in_pad, ffn_tn = dims
    dn_w = dn_h * DN_HEAD_DIM
    n_chunks = S // CHUNK
    mem_n = rms_norm(mem, p["mem_norm"], EPS)
    h = x
    ffn_blocks = (2 * d_ff) // ffn_tn
    half_blocks = ffn_blocks // 2

    def ffn_perm(nb):
        return jnp.where(nb < half_blocks, 2 * nb, 2 * (nb - half_blocks) + 1)

    def ffn_layout(a):
        k = a.shape[0]
        return jnp.transpose(a.reshape(k, 2, half_blocks, ffn_tn), (0, 2, 1, 3)).reshape(k, 2 * d_ff)

    for l in range(depth):
        u = rms_norm(h, p["norm_mix"][l], EPS)
        projp = linear(u, p["w_in"][l])
        proj = jnp.concatenate([projp[:, g * c_in_pad:g * c_in_pad + c_in] for g in range(N_CHIPS)], axis=1)
        o0 = 3 * dn_w
        dz = proj[:, o0:o0 + dn_w]
        db = proj[:, o0 + dn_w:o0 + dn_w + dn_h]
        da = proj[:, o0 + dn_w + dn_h:o0 + dn_w + 2 * dn_h]
        o1 = o0 + dn_w + 2 * dn_h
        mq = proj[:, o1:o1 + MLA_Q_RANK]
        mkv = proj[:, o1 + MLA_Q_RANK:]
        qkv = conv_silu(proj[:, :o0], p["dn_conv"][l])
        qd = l2_norm(qkv[:, :dn_w].reshape(S * dn_h, DN_HEAD_DIM)).reshape(S, dn_w)
        kd = l2_norm(qkv[:, dn_w:2 * dn_w].reshape(S * dn_h, DN_HEAD_DIM)).reshape(S, dn_w)
        vd = qkv[:, 2 * dn_w:]
        beta = jax.nn.sigmoid(db)
        g = -jnp.exp(p["dn_a_log"][l]) * jax.nn.softplus(da + p["dn_dt_bias"][l])
        g_t, beta_t = g.T, beta.T
        o_dn = delta_rule(_heads(qd, dn_h), _heads(kd, dn_h), _heads(vd, dn_h), g_t[:, :, None],
                          g_t.reshape(dn_h, n_chunks, 1, CHUNK), beta_t[:, :, None])
        o_dn = _unheads(o_dn)
        o_dn = rms_norm(o_dn.reshape(S * dn_h, DN_HEAD_DIM), p["dn_out_norm"][l], EPS).reshape(S, dn_w)
        o_dn = o_dn * jax.nn.silu(dz)
        qf = linear(rms_norm(mq, p["mla_q_norm"][l], EPS), p["mla_w_qb"][l]).reshape(S, mla_h, MLA_NOPE + MLA_ROPE)
        q_pe = _rope(qf[..., MLA_NOPE:], cos[:, None, :], sin[:, None, :])
        k_pe = _rope(mkv[:, MLA_KV_RANK:], cos, sin)
        kv = linear(rms_norm(mkv[:, :MLA_KV_RANK], p["mla_kv_norm"][l], EPS), p["mla_w_kvb"][l])
        kv = kv.reshape(S, mla_h, MLA_NOPE + MLA_V)
        qa = jnp.transpose(jnp.concatenate([qf[..., :MLA_NOPE], q_pe], axis=-1), (1, 0, 2))
        ka = jnp.transpose(jnp.concatenate(
            [kv[..., :MLA_NOPE], jnp.broadcast_to(k_pe[:, None, :], (S, mla_h, MLA_ROPE))], axis=-1), (1, 0, 2))
        va = jnp.transpose(kv[..., MLA_NOPE:], (1, 0, 2))
        o_mla = _unheads(attention(qa, ka, va, (MLA_NOPE + MLA_ROPE) ** -0.5, True))
        h = linear(jnp.concatenate([o_dn, o_mla], axis=-1), p["w_out"][l], res=h)
        hn = rms_norm(h, p["norm_xattn"][l], EPS)
        xq = _heads(linear(hn, p["xa_wq"][l]), XA_HEADS)
        xk = _heads(linear(mem_n, p["xa_wk"][l]), XA_HEADS)
        xv = _heads(linear(mem_n, p["xa_wv"][l]), XA_HEADS)
        xo = _unheads(attention(xq, xk, xv, (D // XA_HEADS) ** -0.5, False))
        h = linear(xo, p["xa_wo"][l], res=h)
        hn = rms_norm(h, p["norm_ffn"][l], EPS)
        pre = linear(hn, p["ffn_w_up"][l], tn=ffn_tn, colperm=ffn_perm)
        act = conv_glu(pre, ffn_layout(p["ffn_conv"][l]), ffn_layout(p["ffn_conv_bias"][l][None, :]), ffn_tn)
        h = linear(act, p["ffn_w_down"][l], res=h)
    y = rms_norm(h, p["norm_final"], EPS)
    return sq_loss(y, target)


def kernel(x, mem, positions, norm_mix, w_in, dn_conv, dn_a_log, dn_dt_bias, dn_out_norm, mla_q_norm, mla_w_qb, mla_kv_norm, mla_w_kvb, w_out, mem_norm, norm_xattn, xa_wq, xa_wk, xa_wv, xa_wo, norm_ffn, ffn_w_up, ffn_conv, ffn_conv_bias, ffn_w_down, norm_final, loss_target, m_norm_mix, m_w_in, m_dn_conv, m_dn_a_log, m_dn_dt_bias, m_dn_out_norm, m_mla_q_norm, m_mla_w_qb, m_mla_kv_norm, m_mla_w_kvb, m_w_out, m_mem_norm, m_norm_xattn, m_xa_wq, m_xa_wk, m_xa_wv, m_xa_wo, m_norm_ffn, m_ffn_w_up, m_ffn_conv, m_ffn_conv_bias, m_ffn_w_down, m_norm_final, v_norm_mix, v_w_in, v_dn_conv, v_dn_a_log, v_dn_dt_bias, v_dn_out_norm, v_mla_q_norm, v_mla_w_qb, v_mla_kv_norm, v_mla_w_kvb, v_w_out, v_mem_norm, v_norm_xattn, v_xa_wq, v_xa_wk, v_xa_wv, v_xa_wo, v_norm_ffn, v_ffn_w_up, v_ffn_conv, v_ffn_conv_bias, v_ffn_w_down, v_norm_final):
    w = dict(norm_mix=norm_mix, w_in=w_in, dn_conv=dn_conv, dn_a_log=dn_a_log, dn_dt_bias=dn_dt_bias, dn_out_norm=dn_out_norm, mla_q_norm=mla_q_norm, mla_w_qb=mla_w_qb, mla_kv_norm=mla_kv_norm, mla_w_kvb=mla_w_kvb, w_out=w_out, mem_norm=mem_norm, norm_xattn=norm_xattn, xa_wq=xa_wq, xa_wk=xa_wk, xa_wv=xa_wv, xa_wo=xa_wo, norm_ffn=norm_ffn, ffn_w_up=ffn_w_up, ffn_conv=ffn_conv, ffn_conv_bias=ffn_conv_bias, ffn_w_down=ffn_w_down, norm_final=norm_final)
    m = dict(norm_mix=m_norm_mix, w_in=m_w_in, dn_conv=m_dn_conv, dn_a_log=m_dn_a_log, dn_dt_bias=m_dn_dt_bias, dn_out_norm=m_dn_out_norm, mla_q_norm=m_mla_q_norm, mla_w_qb=m_mla_w_qb, mla_kv_norm=m_mla_kv_norm, mla_w_kvb=m_mla_w_kvb, w_out=m_w_out, mem_norm=m_mem_norm, norm_xattn=m_norm_xattn, xa_wq=m_xa_wq, xa_wk=m_xa_wk, xa_wv=m_xa_wv, xa_wo=m_xa_wo, norm_ffn=m_norm_ffn, ffn_w_up=m_ffn_w_up, ffn_conv=m_ffn_conv, ffn_conv_bias=m_ffn_conv_bias, ffn_w_down=m_ffn_w_down, norm_final=m_norm_final)
    v = dict(norm_mix=v_norm_mix, w_in=v_w_in, dn_conv=v_dn_conv, dn_a_log=v_dn_a_log, dn_dt_bias=v_dn_dt_bias, dn_out_norm=v_dn_out_norm, mla_q_norm=v_mla_q_norm, mla_w_qb=v_mla_w_qb, mla_kv_norm=v_mla_kv_norm, mla_w_kvb=v_mla_w_kvb, w_out=v_w_out, mem_norm=v_mem_norm, norm_xattn=v_norm_xattn, xa_wq=v_xa_wq, xa_wk=v_xa_wk, xa_wv=v_xa_wv, xa_wo=v_xa_wo, norm_ffn=v_norm_ffn, ffn_w_up=v_ffn_w_up, ffn_conv=v_ffn_conv, ffn_conv_bias=v_ffn_conv_bias, ffn_w_down=v_ffn_w_down, norm_final=v_norm_final)

    S, D = x.shape[1], x.shape[2]
    depth = w_in.shape[0]
    dn_h = (D // 2) // DN_HEAD_DIM
    mla_h = (D - dn_h * DN_HEAD_DIM) // MLA_V
    d_ff = ffn_w_down.shape[1] * N_CHIPS
    c_in = w_in.shape[2]
    c_in_pad = -(-c_in // LANES) * LANES
    ffn_tn = _pick(ffn_w_up.shape[2], (256, 128))
    dims = (depth, dn_h, mla_h, d_ff, c_in, c_in_pad, ffn_tn)
    chip = 2 * lax.axis_index("x") + lax.axis_index("y")
    c_idx = lax.axis_index("c").astype(jnp.int32).reshape(1)

    gathered = {n: [] for n in BIG}
    for l in range(depth):
        shards = []
        for n in BIG:
            s = w[n][l].astype(BF16)
            if n == "w_in":
                s = _pad_cols(s, c_in_pad)
            shards.append(s)
        for n, own, full in zip(BIG, shards, gather_weights(shards)):
            full = lax.dynamic_update_slice(full, own[None], (chip, 0, 0))
            if n not in COL_SHARDED:
                full = full.reshape(1, N_CHIPS * full.shape[1], full.shape[2])
            gathered[n].append(full)
    small_sh = small_allgather(_pack([w[n] for n in SMALL_SHARDED]))
    params = dict(gathered)
    for n, parts in zip(SMALL_SHARDED, zip(*[_unpack(small_sh[2 * j], [w[k] for k in SMALL_SHARDED])
                                              for j in range(N_CHIPS)])):
        params[n] = jnp.concatenate(parts, axis=-1)
    for n in SMALL_REPL:
        params[n] = w[n]

    inv = ROPE_BASE ** (-jnp.arange(0, MLA_ROPE, 2, dtype=F32) / MLA_ROPE)
    ang = positions[0].astype(F32)[:, None] * inv
    cos, sin = jnp.cos(ang), jnp.sin(ang)
    loss_fn = lambda p, xx: _forward_loss(p, xx, mem[0], cos, sin, loss_target[0], dims)
    loss_local, (gp, gx) = jax.value_and_grad(loss_fn, argnums=(0, 1))(params, x[0])

    small_names = SMALL_REPL + SMALL_SHARDED
    packed = _pack([loss_local.reshape(1)] + [gp[n] for n in small_names])
    summed = sum_blocks(small_allgather(packed))
    unpacked = _unpack(summed, [loss_local.reshape(1)] + [gp[n] for n in small_names])
    loss = unpacked[0][0]
    grads = dict(zip(small_names, unpacked[1:]))
    for n in SMALL_SHARDED:
        cs = w[n].shape[-1]
        grads[n] = lax.dynamic_slice_in_dim(grads[n], chip * cs, cs, axis=-1)

    chip_idx = chip.astype(jnp.int32).reshape(1)
    accs = [None] * len(BIG)
    for l in range(depth):
        gl = []
        for n in BIG:
            g = gp[n][l]
            if n not in COL_SHARDED:
                g = g.reshape(N_CHIPS, g.shape[1] // N_CHIPS, g.shape[2])
            gl.append(g)
        accs = reduce_layer(gl, c_idx, chip_idx, accs, l, depth)
    theirs = sibling_swap(accs)

    delta, new_m, new_v = {}, {}, {}
    for n, mine, other in zip(BIG, accs, theirs):
        if n == "w_in":
            lo, hi = jnp.where(c_idx[0] == 0, mine, other), jnp.where(c_idx[0] == 0, other, mine)
            grads[n] = jnp.concatenate([lo, hi], axis=1)[:, :, :c_in]
            delta[n], new_m[n], new_v[n] = adamw(w[n], grads[n], m[n], v[n])
        else:
            grads[n], delta[n], new_m[n], new_v[n] = adamw_halves(w[n], mine, other, m[n], v[n], c_idx)
    sw, sg, sm, sv = (_pack([d[n] for n in small_names]) for d in (w, grads, m, v))
    sd, snm, snv = adamw(sw, sg, sm, sv)
    like = [w[n] for n in small_names]
    for d, buf in ((delta, sd), (new_m, snm), (new_v, snv)):
        for n, a in zip(small_names, _unpack(buf, like)):
            d[n] = a

    return (loss, gx[None], *[grads[n] for n in WEIGHTS], *[delta[n] for n in WEIGHTS],
            *[new_m[n] for n in WEIGHTS], *[new_v[n] for n in WEIGHTS])
```

```python
import functools

import numpy as np
import jax
import jax.numpy as jnp
from jax import lax
from jax.experimental import pallas as pl
from jax.experimental.pallas import tpu as pltpu

F32 = jnp.float32
BF16 = jnp.bfloat16
MESH = pl.DeviceIdType.MESH
HIGHEST = lax.Precision.HIGHEST

LANES = 128
VMEM_LIMIT = 56 * 1024 * 1024
N_CHIPS = 4
N_DEV = 8

CHUNK = 64
DN_HEAD_DIM = 128
MLA_NOPE, MLA_ROPE, MLA_V = 128, 64, 128
MLA_Q_RANK, MLA_KV_RANK = 512, 256
XA_HEADS = 4
ROPE_BASE = 10000.0
EPS = 1e-6

ADAM_LR, ADAM_B1, ADAM_B2, ADAM_EPS, ADAM_WD, ADAM_STEP = 0.001, 0.9, 0.999, 1e-08, 0.01, 10

NN = (((1,), (0,)), ((), ()))
NT = (((1,), (1,)), ((), ()))
TN = (((0,), (0,)), ((), ()))


def _pick(dim, cands):
    for c in cands:
        if c <= dim and dim % c == 0:
            return c
    return dim


def _params(sem=None):
    return pltpu.CompilerParams(dimension_semantics=sem, vmem_limit_bytes=VMEM_LIMIT)


def _mm_call(name, a, b, res, dims, grid, a_spec, b_spec, o_spec, out_shape, acc_shape):
    nk = grid[3]
    has_res = res is not None

    def body(*refs):
        if has_res:
            a_ref, b_ref, r_ref, o_ref, acc = refs
        else:
            a_ref, b_ref, o_ref, acc = refs
        kk = pl.program_id(3)

        @pl.when(kk == 0)
        def _():
            acc[...] = jnp.zeros_like(acc)

        acc[...] += lax.dot_general(a_ref[...].astype(BF16), b_ref[...].astype(BF16), dims,
                                    preferred_element_type=F32)

        @pl.when(kk == nk - 1)
        def _():
            r = acc[...]
            if has_res:
                r = r + r_ref[...]
            o_ref[...] = r.astype(o_ref.dtype)

    in_specs = [a_spec, b_spec] + ([o_spec] if has_res else [])
    args = (a, b) + ((res,) if has_res else ())
    return pl.pallas_call(
        body, out_shape=out_shape, grid=grid, in_specs=in_specs, out_specs=o_spec,
        scratch_shapes=[pltpu.VMEM(acc_shape, F32)], name=name,
        compiler_params=_params(("parallel", "parallel", "parallel", "arbitrary")),
    )(*args)


_TM = (1024, 512, 256, 128, 64, 32, 16, 8)
_TN = (1024, 768, 640, 512, 384, 256, 128)


def _ident(nb):
    return nb


def mm_nn_raw(a, w, res=None, tn=None, colperm=_ident, name="mm_nn"):
    M, K = a.shape
    G, _, C = w.shape
    tm = _pick(M, _TM)
    tn = tn or _pick(C, _TN)
    tk = K if K <= 2048 else _pick(K, (1408, 1024, 512, 256, 128))
    nj = C // tn
    grid = (G, M // tm, nj, K // tk)
    return _mm_call(
        name, a, w, res, NN, grid,
        pl.BlockSpec((tm, tk), lambda g, i, j, k: (i, k)),
        pl.BlockSpec((None, tk, tn), lambda g, i, j, k: (g, k, j)),
        pl.BlockSpec((tm, tn), lambda g, i, j, k: (i, colperm(g * nj + j))),
        jax.ShapeDtypeStruct((M, G * C), F32), (tm, tn))


def mm_nt_raw(d, w, tn=None, colperm=_ident, name="mm_nt"):
    M = d.shape[0]
    G, K, C = w.shape
    tm = _pick(M, _TM)
    tko = _pick(K, _TN)
    tc = tn or _pick(C, (2048, 1408, 1280, 1024, 512, 384, 256, 128))
    ncb = C // tc
    grid = (1, M // tm, K // tko, G * ncb)
    return _mm_call(
        name, d, w, None, NT, grid,
        pl.BlockSpec((tm, tc), lambda g, i, j, k: (i, colperm(k))),
        pl.BlockSpec((None, tko, tc), lambda g, i, j, k: (k // ncb, j, k % ncb)),
        pl.BlockSpec((tm, tko), lambda g, i, j, k: (i, j)),
        jax.ShapeDtypeStruct((M, K), F32), (tm, tko))


def mm_tn_raw(at, d, G, tn=None, colperm=_ident, name="mm_tn"):
    K, M = at.shape
    C = d.shape[1] // G
    tko = _pick(K, _TM)
    tn = tn or _pick(C, _TN)
    tm = M if M <= 2048 else _pick(M, (2048, 1024, 512, 256, 128))
    nj = C // tn
    grid = (G, K // tko, nj, M // tm)
    return _mm_call(
        name, at, d, None, NN, grid,
        pl.BlockSpec((tko, tm), lambda g, i, j, k: (i, k)),
        pl.BlockSpec((tm, tn), lambda g, i, j, k: (k, colperm(g * nj + j))),
        pl.BlockSpec((None, tko, tn), lambda g, i, j, k: (g, i, j)),
        jax.ShapeDtypeStruct((G, K, C), BF16), (tko, tn))


def _make_linear(tn, colperm, with_res):
    @jax.custom_vjp
    def lin(a, w, res):
        return mm_nn_raw(a.astype(BF16), w, res if with_res else None, tn=tn, colperm=colperm)

    def fwd(a, w, res):
        a16 = a.astype(BF16)
        return mm_nn_raw(a16, w, res if with_res else None, tn=tn, colperm=colperm), (a16, w)

    def bwd(saved, dout):
        a16, w = saved
        d16 = dout.astype(BF16)
        da = mm_nt_raw(d16, w, tn=tn, colperm=colperm)
        dw = mm_tn_raw(a16.T, d16, w.shape[0], tn=tn, colperm=colperm)
        return da, dw, (dout if with_res else None)

    lin.defvjp(fwd, bwd)
    return lin


def linear(a, w, res=None, tn=None, colperm=_ident):
    if res is None:
        return _make_linear(tn, colperm, False)(a, w, None)
    return _make_linear(tn, colperm, True)(a, w, res)


def _row_tile(R, D):
    return _pick(R, tuple(t for t in (2048, 1024, 512, 256, 128, 64, 32, 16, 8) if t * D * 4 <= (2 << 20)))


def _norm_fwd_call(x, gain, eps, inv_n):
    R, D = x.shape
    tr = _row_tile(R, D)
    has_gain = gain is not None

    def body(*refs):
        if has_gain:
            x_ref, g_ref, y_ref = refs
        else:
            x_ref, y_ref = refs
        xv = x_ref[...]
        r = lax.rsqrt(jnp.sum(xv * xv, axis=-1, keepdims=True) * inv_n + eps)
        y = xv * r
        if has_gain:
            y = y * g_ref[...]
        y_ref[...] = y

    row = pl.BlockSpec((tr, D), lambda i: (i, 0))
    in_specs = [row] + ([pl.BlockSpec((1, D), lambda i: (0, 0))] if has_gain else [])
    args = (x,) + ((gain.reshape(1, D),) if has_gain else ())
    return pl.pallas_call(body, out_shape=jax.ShapeDtypeStruct((R, D), F32), grid=(R // tr,),
                          in_specs=in_specs, out_specs=row, name="norm_fwd",
                          compiler_params=_params(("parallel",)))(*args)


def _norm_bwd_call(x, gain, dy, eps, inv_n):
    R, D = x.shape
    tr = _row_tile(R, D)
    has_gain = gain is not None

    def body(*refs):
        if has_gain:
            x_ref, g_ref, dy_ref, dx_ref, dg_ref = refs
        else:
            x_ref, dy_ref, dx_ref = refs
        xv = x_ref[...]
        dyv = dy_ref[...]
        r = lax.rsqrt(jnp.sum(xv * xv, axis=-1, keepdims=True) * inv_n + eps)
        xh = xv * r
        dxh = dyv * g_ref[...] if has_gain else dyv
        dx_ref[...] = r * (dxh - xh * (inv_n * jnp.sum(dxh * xh, axis=-1, keepdims=True)))
        if has_gain:
            @pl.when(pl.program_id(0) == 0)
            def _():
                dg_ref[...] = jnp.zeros_like(dg_ref)

            dg_ref[...] += jnp.sum(dyv * xh, axis=0, keepdims=True)

    row = pl.BlockSpec((tr, D), lambda i: (i, 0))
    vec = pl.BlockSpec((1, D), lambda i: (0, 0))
    if has_gain:
        dx, dg = pl.pallas_call(
            body, out_shape=(jax.ShapeDtypeStruct((R, D), F32), jax.ShapeDtypeStruct((1, D), F32)),
            grid=(R // tr,), in_specs=[row, vec, row], out_specs=(row, vec), name="norm_bwd",
            compiler_params=_params(("arbitrary",)))(x, gain.reshape(1, D), dy)
        return dx, dg.reshape(D)
    dx = pl.pallas_call(body, out_shape=jax.ShapeDtypeStruct((R, D), F32), grid=(R // tr,),
                        in_specs=[row, row], out_specs=row, name="l2norm_bwd",
                        compiler_params=_params(("parallel",)))(x, dy)
    return dx, None


@functools.partial(jax.custom_vjp, nondiff_argnums=(2,))
def rms_norm(x, gain, eps):
    return _norm_fwd_call(x, gain, eps, 1.0 / x.shape[-1])


def _rms_norm_fwd(x, gain, eps):
    return rms_norm(x, gain, eps), (x, gain)


def _rms_norm_bwd(eps, saved, dy):
    x, gain = saved
    return _norm_bwd_call(x, gain, dy, eps, 1.0 / x.shape[-1])


rms_norm.defvjp(_rms_norm_fwd, _rms_norm_bwd)


@jax.custom_vjp
def l2_norm(x):
    return _norm_fwd_call(x, None, EPS, 1.0)


def _l2_norm_fwd(x):
    return l2_norm(x), x


def _l2_norm_bwd(x, dy):
    return (_norm_bwd_call(x, None, dy, EPS, 1.0)[0],)


l2_norm.defvjp(_l2_norm_fwd, _l2_norm_bwd)


def _attn_mask(tq, sk):
    q0 = pl.program_id(1) * tq
    qc = (q0 + lax.broadcasted_iota(jnp.int32, (tq, sk), 0)) // CHUNK
    kc = lax.broadcasted_iota(jnp.int32, (tq, sk), 1) // CHUNK
    return kc <= qc


def _attn_dims(q, k, v, heads):
    if heads is None:
        H, S, dk = q.shape
        return H, S, dk, k.shape[1], v.shape[2]
    return heads, q.shape[0], q.shape[1] // heads, k.shape[0], v.shape[1] // heads


def _attn_specs(heads, tq, dk, Sk, dv):
    if heads is None:
        return (pl.BlockSpec((None, tq, dk), lambda h, i: (h, i, 0)), pl.BlockSpec((None, Sk, dk), lambda h, i: (h, 0, 0)),
                pl.BlockSpec((None, Sk, dv), lambda h, i: (h, 0, 0)), pl.BlockSpec((None, tq, dv), lambda h, i: (h, i, 0)))
    return (pl.BlockSpec((tq, dk), lambda h, i: (i, h)), pl.BlockSpec((Sk, dk), lambda h, i: (0, h)),
            pl.BlockSpec((Sk, dv), lambda h, i: (0, h)), pl.BlockSpec((tq, dv), lambda h, i: (i, h)))


def _attn_fwd_call(q, k, v, scale, causal, heads):
    H, S, dk, Sk, dv = _attn_dims(q, k, v, heads)
    tq = _pick(S, (256, 128, 64))
    qs, ks, vs, os_ = _attn_specs(heads, tq, dk, Sk, dv)

    def body(q_ref, k_ref, v_ref, o_ref, lse_ref):
        s = lax.dot_general(q_ref[...].astype(BF16), k_ref[...].astype(BF16), NT,
                            preferred_element_type=F32) * scale
        if causal:
            s = jnp.where(_attn_mask(tq, Sk), s, -1e30)
        m = jnp.max(s, axis=-1, keepdims=True)
        p = jnp.exp(s - m)
        l = jnp.sum(p, axis=-1, keepdims=True)
        o = lax.dot_general(p.astype(BF16), v_ref[...].astype(BF16), NN, preferred_element_type=F32)
        o_ref[...] = o / l
        lse_ref[...] = m + jnp.log(l)

    o_shape = (H, S, dv) if heads is None else (S, H * dv)
    return pl.pallas_call(
        body,
        out_shape=(jax.ShapeDtypeStruct(o_shape, F32), jax.ShapeDtypeStruct((H, S, 1), F32)),
        grid=(H, S // tq), in_specs=[qs, ks, vs],
        out_specs=(os_, pl.BlockSpec((None, tq, 1), lambda h, i: (h, i, 0))),
        name="attn_fwd", compiler_params=_params(("parallel", "parallel")))(q, k, v)


def _attn_bwd_call(q, k, v, o, lse, do, scale, causal, heads):
    H, S, dk, Sk, dv = _attn_dims(q, k, v, heads)
    tq = _pick(S, (256, 128, 64))
    qs, ks, vs, os_ = _attn_specs(heads, tq, dk, Sk, dv)

    def body(q_ref, k_ref, v_ref, o_ref, lse_ref, do_ref, dq_ref, dk_ref, dv_ref):
        qb = q_ref[...].astype(BF16)
        kb = k_ref[...].astype(BF16)
        dob = do_ref[...].astype(BF16)
        s = lax.dot_general(qb, kb, NT, preferred_element_type=F32) * scale
        if causal:
            s = jnp.where(_attn_mask(tq, Sk), s, -1e30)
        p = jnp.exp(s - lse_ref[...])
        dp = lax.dot_general(dob, v_ref[...].astype(BF16), NT, preferred_element_type=F32)
        delta = jnp.sum(do_ref[...] * o_ref[...], axis=-1, keepdims=True)
        ds = (p * (dp - delta) * scale).astype(BF16)
        dq_ref[...] = lax.dot_general(ds, kb, NN, preferred_element_type=F32)

        @pl.when(pl.program_id(1) == 0)
        def _():
            dk_ref[...] = jnp.zeros_like(dk_ref)
            dv_ref[...] = jnp.zeros_like(dv_ref)

        dk_ref[...] += lax.dot_general(ds, qb, TN, preferred_element_type=F32)
        dv_ref[...] += lax.dot_general(p.astype(BF16), dob, TN, preferred_element_type=F32)

    ls = pl.BlockSpec((None, tq, 1), lambda h, i: (h, i, 0))
    return pl.pallas_call(
        body,
        out_shape=(jax.ShapeDtypeStruct(q.shape, F32), jax.ShapeDtypeStruct(k.shape, F32),
                   jax.ShapeDtypeStruct(v.shape, F32)),
        grid=(H, S // tq), in_specs=[qs, ks, vs, os_, ls, os_], out_specs=(qs, ks, vs),
        name="attn_bwd", compiler_params=_params(("parallel", "arbitrary")))(q, k, v, o, lse, do)


@functools.partial(jax.custom_vjp, nondiff_argnums=(3, 4, 5))
def attention(q, k, v, scale, causal, heads=None):
    return _attn_fwd_call(q, k, v, scale, causal, heads)[0]


def _attention_fwd(q, k, v, scale, causal, heads):
    o, lse = _attn_fwd_call(q, k, v, scale, causal, heads)
    return o, (q, k, v, o, lse)


def _attention_bwd(scale, causal, heads, saved, do):
    q, k, v, o, lse = saved
    return _attn_bwd_call(q, k, v, o, lse, do, scale, causal, heads)


attention.defvjp(_attention_fwd, _attention_bwd)


NNB = (((2,), (1,)), ((0,), (0,)))
NTB = (((2,), (2,)), ((0,), (0,)))
TNB = (((1,), (1,)), ((0,), (0,)))


def _dot3(a, b, dims):
    a_hi, b_hi = a.astype(BF16), b.astype(BF16)
    a_lo = (a - a_hi.astype(F32)).astype(BF16)
    b_lo = (b - b_hi.astype(F32)).astype(BF16)

    def dot(x, y):
        return lax.dot_general(x, y, dims, preferred_element_type=F32)

    return dot(a_hi, b_hi) + (dot(a_hi, b_lo) + dot(a_lo, b_hi))


@functools.partial(jax.custom_vjp, nondiff_argnums=(2,))
def _hdot(a, b, dims):
    return _dot3(a, b, dims)


def _hdot_fwd(a, b, dims):
    return _dot3(a, b, dims), (a, b)


def _hdot_bwd(dims, saved, g):
    a, b = saved
    if dims == NNB:
        return _dot3(g, b, NTB), _dot3(a, g, TNB)
    if dims == NTB:
        return _dot3(g, b, NNB), _dot3(g, a, TNB)
    assert dims == TNB
    return _dot3(b, g, NTB), _dot3(a, g, NNB)


_hdot.defvjp(_hdot_fwd, _hdot_bwd)


def _dn_chunk(q, k, v, gcol, grow, bcol, state):
    c = q.shape[1]
    r_i = lax.broadcasted_iota(jnp.int32, (1, c, c), 1)
    c_i = lax.broadcasted_iota(jnp.int32, (1, c, c), 2)
    incl = c_i <= r_i
    strict = c_i < r_i
    g_cum_col = jnp.sum(jnp.where(incl, grow, 0.0), axis=2, keepdims=True)
    g_cum_row = jnp.sum(jnp.where(r_i <= c_i, gcol, 0.0), axis=1, keepdims=True)
    g_last = jnp.sum(grow, axis=2, keepdims=True)
    decay = jnp.where(incl, jnp.exp(jnp.where(incl, g_cum_col - g_cum_row, 0.0)), 0.0)
    qs = q * (q.shape[2] ** -0.5)
    kb = k * bcol
    x = -jnp.where(strict, _hdot(kb, k, NTB) * decay, 0.0)
    eye = (r_i == c_i).astype(F32)
    t = eye + x
    steps = max(1, int(np.ceil(np.log2(c))) - 1)
    for _ in range(steps):
        x = _hdot(x, x, NNB)
        t = t + _hdot(t, x, NNB)
    e_col = jnp.exp(g_cum_col)
    u = _hdot(t, v * bcol, NNB)
    w = _hdot(t, kb * e_col, NNB)
    attn = _hdot(qs, k, NTB) * decay
    v_new = u - _hdot(w, state, NNB)
    o = _hdot(qs * e_col, state, NNB) + _hdot(attn, v_new, NNB)
    k_dec = k * jnp.exp(g_last - g_cum_col)
    new_state = state * jnp.exp(g_last) + _hdot(k_dec, v_new, TNB)
    return o, new_state


DN_HEADS_PER_STEP = 8


def _dn_specs(hb, D, chunk_index):
    seq = pl.BlockSpec((CHUNK, hb * D), lambda h, n: (chunk_index(n), h))
    col = pl.BlockSpec((hb, CHUNK, 1), lambda h, n: (h, chunk_index(n), 0))
    row = pl.BlockSpec((hb, None, 1, CHUNK), lambda h, n: (h, chunk_index(n), 0, 0))
    st = pl.BlockSpec((hb, None, D, D), lambda h, n: (h, chunk_index(n), 0, 0))
    return seq, col, row, st


def _dn_heads(ref, hb, D):
    return jnp.stack([ref[:, h * D:(h + 1) * D] for h in range(hb)])


def _dn_fwd_call(q, k, v, gcol, grow, bcol):
    H = gcol.shape[0]
    S, D = q.shape[0], q.shape[1] // H
    N = S // CHUNK
    hb = _pick(H, (DN_HEADS_PER_STEP, 2, 1))
    seq, col, row, st = _dn_specs(hb, D, lambda n: n)

    def body(q_ref, k_ref, v_ref, gc_ref, gr_ref, bc_ref, o_ref, st_ref, state):
        @pl.when(pl.program_id(1) == 0)
        def _():
            state[...] = jnp.zeros_like(state)

        s_in = state[...]
        st_ref[...] = s_in
        o, s_out = _dn_chunk(_dn_heads(q_ref, hb, D), _dn_heads(k_ref, hb, D), _dn_heads(v_ref, hb, D),
                             gc_ref[...], gr_ref[...], bc_ref[...], s_in)
        for h in range(hb):
            o_ref[:, h * D:(h + 1) * D] = o[h]
        state[...] = s_out

    return pl.pallas_call(
        body,
        out_shape=(jax.ShapeDtypeStruct((S, H * D), F32), jax.ShapeDtypeStruct((H, N, D, D), F32)),
        grid=(H // hb, N), in_specs=[seq, seq, seq, col, row, col], out_specs=(seq, st),
        scratch_shapes=[pltpu.VMEM((hb, D, D), F32)], name="deltanet_fwd",
        compiler_params=_params(("parallel", "arbitrary")))(q, k, v, gcol, grow, bcol)


def _dn_bwd_call(q, k, v, gcol, grow, bcol, states, do):
    H = gcol.shape[0]
    S, D = q.shape[0], q.shape[1] // H
    N = S // CHUNK
    hb = _pick(H, (DN_HEADS_PER_STEP, 2, 1))
    seq, col, row, st = _dn_specs(hb, D, lambda n: N - 1 - n)

    def body(q_ref, k_ref, v_ref, gc_ref, gr_ref, bc_ref, st_ref, do_ref,
             dq_ref, dk_ref, dv_ref, dgc_ref, dgr_ref, dbc_ref, dstate):
        @pl.when(pl.program_id(1) == 0)
        def _():
            dstate[...] = jnp.zeros_like(dstate)

        _, vjp = jax.vjp(_dn_chunk, _dn_heads(q_ref, hb, D), _dn_heads(k_ref, hb, D), _dn_heads(v_ref, hb, D),
                         gc_ref[...], gr_ref[...], bc_ref[...], st_ref[...])
        grads = vjp((_dn_heads(do_ref, hb, D), dstate[...]))
        for ref, val in zip((dq_ref, dk_ref, dv_ref), grads[:3]):
            for h in range(hb):
                ref[:, h * D:(h + 1) * D] = val[h]
        for ref, val in zip((dgc_ref, dgr_ref, dbc_ref, dstate), grads[3:]):
            ref[...] = val

    sd = jax.ShapeDtypeStruct
    return pl.pallas_call(
        body,
        out_shape=(sd((S, H * D), F32), sd((S, H * D), F32), sd((S, H * D), F32),
                   sd((H, S, 1), F32), sd((H, N, 1, CHUNK), F32), sd((H, S, 1), F32)),
        grid=(H // hb, N), in_specs=[seq, seq, seq, col, row, col, st, seq],
        out_specs=(seq, seq, seq, col, row, col),
        scratch_shapes=[pltpu.VMEM((hb, D, D), F32)], name="deltanet_bwd",
        compiler_params=_params(("parallel", "arbitrary")))(q, k, v, gcol, grow, bcol, states, do)


@jax.custom_vjp
def delta_rule(q, k, v, gcol, grow, bcol):
    return _dn_fwd_call(q, k, v, gcol, grow, bcol)[0]


def _delta_rule_fwd(q, k, v, gcol, grow, bcol):
    o, states = _dn_fwd_call(q, k, v, gcol, grow, bcol)
    return o, (q, k, v, gcol, grow, bcol, states)


def _delta_rule_bwd(saved, do):
    return _dn_bwd_call(*saved, do)


delta_rule.defvjp(_delta_rule_fwd, _delta_rule_bwd)


def _shift_down(x, j):
    if j == 0:
        return x
    rows = lax.broadcasted_iota(jnp.int32, x.shape, 0)
    return jnp.where(rows >= j, pltpu.roll(x, j, 0), 0.0)


def _shift_up(x, j):
    if j == 0:
        return x
    s = x.shape[0]
    rows = lax.broadcasted_iota(jnp.int32, x.shape, 0)
    return jnp.where(rows < s - j, pltpu.roll(x, s - j, 0), 0.0)


def _conv(x, w):
    kw = w.shape[0]
    acc = x * w[kw - 1:kw, :]
    for kk in range(kw - 1):
        acc = acc + _shift_down(x, kw - 1 - kk) * w[kk:kk + 1, :]
    return acc


def _conv_t(d, w):
    kw = w.shape[0]
    acc = d * w[kw - 1:kw, :]
    for kk in range(kw - 1):
        acc = acc + _shift_up(d, kw - 1 - kk) * w[kk:kk + 1, :]
    return acc


def _conv_dw_rows(d, x, kw):
    return [jnp.sum(d * _shift_down(x, kw - 1 - kk), axis=0, keepdims=True) for kk in range(kw)]


def _silu(a):
    return a * jax.nn.sigmoid(a)


def _dsilu(a):
    s = jax.nn.sigmoid(a)
    return s * (1.0 + a * (1.0 - s))


def _conv_silu_fwd_call(x, w):
    S, C = x.shape
    kw = w.shape[0]
    tc = _pick(C, (256, 128))

    def body(x_ref, w_ref, y_ref):
        y_ref[...] = _silu(_conv(x_ref[...], w_ref[...]))

    xs = pl.BlockSpec((S, tc), lambda j: (0, j))
    ws = pl.BlockSpec((kw, tc), lambda j: (0, j))
    return pl.pallas_call(body, out_shape=jax.ShapeDtypeStruct((S, C), F32), grid=(C // tc,),
                          in_specs=[xs, ws], out_specs=xs, name="conv_silu_fwd",
                          compiler_params=_params(("parallel",)))(x, w)


def _conv_silu_bwd_call(x, w, dy):
    S, C = x.shape
    kw = w.shape[0]
    tc = _pick(C, (256, 128))

    def body(x_ref, w_ref, dy_ref, dx_ref, dw_ref):
        xv = x_ref[...]
        wv = w_ref[...]
        da = dy_ref[...] * _dsilu(_conv(xv, wv))
        dx_ref[...] = _conv_t(da, wv)
        for kk, row in enumerate(_conv_dw_rows(da, xv, kw)):
            dw_ref[kk:kk + 1, :] = row

    xs = pl.BlockSpec((S, tc), lambda j: (0, j))
    ws = pl.BlockSpec((kw, tc), lambda j: (0, j))
    return pl.pallas_call(
        body, out_shape=(jax.ShapeDtypeStruct((S, C), F32), jax.ShapeDtypeStruct((kw, C), F32)),
        grid=(C // tc,), in_specs=[xs, ws, xs], out_specs=(xs, ws), name="conv_silu_bwd",
        compiler_params=_params(("parallel",)))(x, w, dy)


@jax.custom_vjp
def conv_silu(x, w):
    return _conv_silu_fwd_call(x, w)


def _conv_silu_fwd(x, w):
    return conv_silu(x, w), (x, w)


def _conv_silu_bwd(saved, dy):
    return _conv_silu_bwd_call(*saved, dy)


conv_silu.defvjp(_conv_silu_fwd, _conv_silu_bwd)


def _conv_glu_fwd_call(x, w, b, tc):
    S, C2 = x.shape
    kw = w.shape[0]
    nb = C2 // (2 * tc)

    def body(x_ref, w_ref, b_ref, y_ref):
        a = _conv(x_ref[...], w_ref[...]) + b_ref[...]
        y_ref[...] = _silu(a[:, :tc]) * a[:, tc:]

    return pl.pallas_call(
        body, out_shape=jax.ShapeDtypeStruct((S, C2 // 2), F32), grid=(nb,),
        in_specs=[pl.BlockSpec((S, 2 * tc), lambda j: (0, j)), pl.BlockSpec((kw, 2 * tc), lambda j: (0, j)),
                  pl.BlockSpec((1, 2 * tc), lambda j: (0, j))],
        out_specs=pl.BlockSpec((S, tc), lambda j: (0, j)), name="conv_glu_fwd",
        compiler_params=_params(("parallel",)))(x, w, b)


def _conv_glu_bwd_call(x, w, b, dy, tc):
    S, C2 = x.shape
    kw = w.shape[0]
    nb = C2 // (2 * tc)

    def body(x_ref, w_ref, b_ref, dy_ref, dx_ref, dw_ref, db_ref):
        xv = x_ref[...]
        wv = w_ref[...]
        a = _conv(xv, wv) + b_ref[...]
        ag, au = a[:, :tc], a[:, tc:]
        dyv = dy_ref[...]
        da = jnp.concatenate([dyv * au * _dsilu(ag), dyv * _silu(ag)], axis=1)
        dx_ref[...] = _conv_t(da, wv)
        for kk, row in enumerate(_conv_dw_rows(da, xv, kw)):
            dw_ref[kk:kk + 1, :] = row
        db_ref[...] = jnp.sum(da, axis=0, keepdims=True)

    xs = pl.BlockSpec((S, 2 * tc), lambda j: (0, j))
    ws = pl.BlockSpec((kw, 2 * tc), lambda j: (0, j))
    bs = pl.BlockSpec((1, 2 * tc), lambda j: (0, j))
    return pl.pallas_call(
        body, out_shape=(jax.ShapeDtypeStruct((S, C2), F32), jax.ShapeDtypeStruct((kw, C2), F32),
                         jax.ShapeDtypeStruct((1, C2), F32)),
        grid=(nb,), in_specs=[xs, ws, bs, pl.BlockSpec((S, tc), lambda j: (0, j))], out_specs=(xs, ws, bs),
        name="conv_glu_bwd", compiler_params=_params(("parallel",)))(x, w, b, dy)


@functools.partial(jax.custom_vjp, nondiff_argnums=(3,))
def conv_glu(x, w, b, tc):
    return _conv_glu_fwd_call(x, w, b, tc)


def _conv_glu_fwd(x, w, b, tc):
    return conv_glu(x, w, b, tc), (x, w, b)


def _conv_glu_bwd(tc, saved, dy):
    return _conv_glu_bwd_call(*saved, dy, tc)


conv_glu.defvjp(_conv_glu_fwd, _conv_glu_bwd)


def _sqerr_call(y, t):
    S, D = y.shape
    tr = _row_tile(S, D)

    def body(y_ref, t_ref, l_ref, dy_ref):
        d = y_ref[...] - t_ref[...]
        dy_ref[...] = d * (1.0 / D)

        @pl.when(pl.program_id(0) == 0)
        def _():
            l_ref[...] = jnp.zeros_like(l_ref)

        l_ref[...] += jnp.full((8, LANES), 0.5 / D, F32) * jnp.sum(d * d)

    row = pl.BlockSpec((tr, D), lambda i: (i, 0))
    one = pl.BlockSpec((8, LANES), lambda i: (0, 0))
    return pl.pallas_call(
        body, out_shape=(jax.ShapeDtypeStruct((8, LANES), F32), jax.ShapeDtypeStruct((S, D), F32)),
        grid=(S // tr,), in_specs=[row, row], out_specs=(one, row), name="loss_head",
        compiler_params=_params(("arbitrary",)))(y, t)


@jax.custom_vjp
def sq_loss(y, t):
    return _sqerr_call(y, t)[0][0, 0]


def _sq_loss_fwd(y, t):
    l, dy = _sqerr_call(y, t)
    return l[0, 0], dy


def _sq_loss_bwd(dy, dl):
    return dy * dl, None


sq_loss.defvjp(_sq_loss_fwd, _sq_loss_bwd)


def adamw(w, g, m, v):
    shape = w.shape
    C = shape[-1]
    R = int(np.prod(shape[:-1]))
    tr = _pick(R, tuple(t for t in (4096, 2048, 1024, 512, 256, 128, 64, 32, 16, 8) if t * C * 4 <= (1 << 20)))
    c1 = 1.0 / (1.0 - ADAM_B1 ** ADAM_STEP)
    c2 = 1.0 / (1.0 - ADAM_B2 ** ADAM_STEP)

    def body(w_ref, g_ref, m_ref, v_ref, d_ref, nm_ref, nv_ref):
        gv = g_ref[...]
        nm = ADAM_B1 * m_ref[...] + (1.0 - ADAM_B1) * gv
        nv = ADAM_B2 * v_ref[...] + (1.0 - ADAM_B2) * (gv * gv)
        d_ref[...] = -ADAM_LR * ((nm * c1) / (jnp.sqrt(nv * c2) + ADAM_EPS) + ADAM_WD * w_ref[...])
        nm_ref[...] = nm
        nv_ref[...] = nv

    blk = pl.BlockSpec((tr, C), lambda i: (i, 0))
    out = jax.ShapeDtypeStruct((R, C), F32)
    d, nm, nv = pl.pallas_call(body, out_shape=(out, out, out), grid=(R // tr,), in_specs=[blk] * 4,
                               out_specs=(blk, blk, blk), name="adamw",
                               compiler_params=_params(("parallel",)))(
        w.reshape(R, C), g.reshape(R, C), m.reshape(R, C), v.reshape(R, C))
    return d.reshape(shape), nm.reshape(shape), nv.reshape(shape)


def _place():
    x, y, c = lax.axis_index("x"), lax.axis_index("y"), lax.axis_index("c")
    chips = [(1 - x, y), (x, 1 - y), (1 - x, 1 - y)]
    return x, y, c, chips


def small_allgather(buf):
    m_per, n = buf.shape

    def body(x_ref, out_ref, send_sems, recv_sems, local_sem):
        x, y, c, chips = _place()
        me, sibling = (x, y, c), (x, y, 1 - c)

        def rows(px, py, pc):
            return out_ref.at[4 * px + 2 * py + pc]

        def copy(k, block, to, src=None):
            return pltpu.make_async_remote_copy(
                src_ref=rows(*block) if src is None else src, dst_ref=rows(*block),
                send_sem=send_sems.at[k], recv_sem=recv_sems.at[k], device_id=to, device_id_type=MESH)

        mine = pltpu.make_async_copy(x_ref, rows(*me), local_sem)
        mine.start()
        first = [copy(0, me, sibling, src=x_ref)]
        first += [copy(1 + j, me, (*chip, c), src=x_ref) for j, chip in enumerate(chips)]
        for cp in first:
            cp.start()
        passed = [copy(4 + j, (*chip, c), sibling) for j, chip in enumerate(chips)]
        for j, chip in enumerate(chips):
            copy(1 + j, (*chip, c), me).wait_recv()
            passed[j].start()
        copy(0, sibling, me).wait_recv()
        for j, chip in enumerate(chips):
            copy(4 + j, (*chip, 1 - c), me).wait_recv()
        for cp in first + passed:
            cp.wait_send()
        mine.wait()

    return pl.pallas_call(
        body, out_shape=jax.ShapeDtypeStruct((N_DEV, m_per, n), buf.dtype),
        in_specs=[pl.BlockSpec(memory_space=pltpu.VMEM)], out_specs=pl.BlockSpec(memory_space=pltpu.VMEM),
        scratch_shapes=[pltpu.SemaphoreType.DMA((7,)), pltpu.SemaphoreType.DMA((7,)), pltpu.SemaphoreType.DMA],
        name="small_allgather", compiler_params=pltpu.CompilerParams(vmem_limit_bytes=VMEM_LIMIT))(buf)


def sum_blocks(g):
    n, R, C = g.shape
    tr = _pick(R, (512, 256, 128, 64, 32, 16, 8))

    def body(g_ref, o_ref):
        acc = g_ref[0].astype(F32)
        for d in range(1, n):
            acc = acc + g_ref[d].astype(F32)
        o_ref[...] = acc

    return pl.pallas_call(body, out_shape=jax.ShapeDtypeStruct((R, C), F32), grid=(R // tr,),
                          in_specs=[pl.BlockSpec((n, tr, C), lambda i: (0, i, 0))],
                          out_specs=pl.BlockSpec((tr, C), lambda i: (i, 0)), name="sum_blocks",
                          compiler_params=_params(("parallel",)))(g)


def _hbm_specs(n):
    return [pl.BlockSpec(memory_space=pl.ANY)] * n


def gather_weights(shards):
    n = len(shards)

    def body(*refs):
        src, out = refs[:n], refs[n:2 * n]
        send_sems, recv_sems = refs[2 * n:]
        x, y, c, chips = _place()
        xn, yn, dg = chips
        sibling = (x, y, 1 - c)
        my_chip = 2 * x + y
        slot = lambda chip: 2 * chip[0] + chip[1]
        started = []

        def copy(t, k, ref, to):
            return pltpu.make_async_remote_copy(
                src_ref=ref, dst_ref=ref, send_sem=send_sems.at[t, k], recv_sem=recv_sems.at[t, k],
                device_id=to, device_id_type=MESH)

        def start(cp):
            cp.start()
            started.append(cp)

        def rows(t, chip, first, count):
            return out[t].at[slot(chip), pl.ds(first, count)]

        for t in range(n):
            half = src[t].shape[0] // 2
            for k, chip in enumerate((xn, yn)):
                cp = pltpu.make_async_remote_copy(
                    src_ref=src[t].at[pl.ds(c * half, half)], dst_ref=out[t].at[my_chip, pl.ds(c * half, half)],
                    send_sem=send_sems.at[t, k], recv_sem=recv_sems.at[t, k], device_id=(*chip, c), device_id_type=MESH)
                start(cp)
        for t in range(n):
            half = src[t].shape[0] // 2
            quarter = half // 2
            copy(t, 0, rows(t, xn, c * half, half), (*xn, c)).wait_recv()
            start(copy(t, 2, rows(t, xn, c * half, quarter), (*yn, c)))
            start(copy(t, 4, rows(t, xn, c * half, half), sibling))
            copy(t, 1, rows(t, yn, c * half, half), (*yn, c)).wait_recv()
            start(copy(t, 3, rows(t, yn, c * half + quarter, quarter), (*xn, c)))
            start(copy(t, 5, rows(t, yn, c * half, half), sibling))
        for t in range(n):
            half = src[t].shape[0] // 2
            quarter = half // 2
            copy(t, 2, rows(t, dg, c * half, quarter), (*yn, c)).wait_recv()
            copy(t, 3, rows(t, dg, c * half + quarter, quarter), (*xn, c)).wait_recv()
            start(copy(t, 6, rows(t, dg, c * half, half), sibling))
        for t in range(n):
            half = src[t].shape[0] // 2
            for k, chip in ((4, xn), (5, yn), (6, dg)):
                copy(t, k, rows(t, chip, (1 - c) * half, half), sibling).wait_recv()
        for cp in started:
            cp.wait_send()

    out_shape = tuple(jax.ShapeDtypeStruct((N_CHIPS,) + s.shape, s.dtype) for s in shards)
    return pl.pallas_call(
        body, out_shape=out_shape, in_specs=_hbm_specs(n), out_specs=tuple(_hbm_specs(n)),
        scratch_shapes=[pltpu.SemaphoreType.DMA((n, 7)), pltpu.SemaphoreType.DMA((n, 7))],
        name="gather_weights", compiler_params=pltpu.CompilerParams(has_side_effects=True))(*shards)


def sibling_exchange_halves(grads):
    n = len(grads)

    def body(*refs):
        src, out = refs[:n], refs[n:2 * n]
        send_sems, recv_sems = refs[2 * n:]
        x, y, c, _ = _place()
        sibling = (x, y, 1 - c)
        cps = []
        for t in range(n):
            half = src[t].shape[1] // 2
            cp = pltpu.make_async_remote_copy(
                src_ref=src[t].at[:, pl.ds((1 - c) * half, half)], dst_ref=out[t],
                send_sem=send_sems.at[t], recv_sem=recv_sems.at[t], device_id=sibling, device_id_type=MESH)
            cp.start()
            cps.append(cp)
        for cp in cps:
            cp.wait()

    out_shape = tuple(jax.ShapeDtypeStruct((s.shape[0], s.shape[1] // 2, s.shape[2]), s.dtype) for s in grads)
    return pl.pallas_call(
        body, out_shape=out_shape, in_specs=_hbm_specs(n), out_specs=tuple(_hbm_specs(n)),
        scratch_shapes=[pltpu.SemaphoreType.DMA((n,)), pltpu.SemaphoreType.DMA((n,))],
        name="sibling_exchange_halves", compiler_params=pltpu.CompilerParams(has_side_effects=True))(*grads)


def pair_sum(g, recv, c_idx):
    G, R, C = g.shape
    half = R // 2
    tr = _pick(half, tuple(t for t in (1024, 512, 256, 128, 64, 32, 16) if t * C * 2 <= (4 << 20)))
    g4 = g.reshape(G, 2, half, C)

    def body(c_ref, g_ref, r_ref, o_ref):
        o_ref[...] = (g_ref[...].astype(F32) + r_ref[...].astype(F32)).astype(o_ref.dtype)

    grid_spec = pltpu.PrefetchScalarGridSpec(
        num_scalar_prefetch=1, grid=(G, half // tr),
        in_specs=[pl.BlockSpec((None, None, tr, C), lambda i, j, c_ref: (i, c_ref[0], j, 0)),
                  pl.BlockSpec((None, tr, C), lambda i, j, c_ref: (i, j, 0))],
        out_specs=pl.BlockSpec((None, tr, C), lambda i, j, c_ref: (i, j, 0)))
    return pl.pallas_call(body, out_shape=jax.ShapeDtypeStruct((G, half, C), g.dtype), grid_spec=grid_spec,
                          name="pair_sum", compiler_params=_params(("parallel", "parallel")))(c_idx, g4, recv)


def chip_exchange(parts):
    n = len(parts)

    def body(*refs):
        src, out = refs[:n], refs[n:2 * n]
        send_sems, recv_sems = refs[2 * n:]
        x, y, c, chips = _place()
        cps = []
        for t in range(n):
            for j, chip in enumerate(chips):
                cp = pltpu.make_async_remote_copy(
                    src_ref=src[t].at[2 * chip[0] + chip[1]], dst_ref=out[t].at[j],
                    send_sem=send_sems.at[t, j], recv_sem=recv_sems.at[t, j],
                    device_id=(*chip, c), device_id_type=MESH)
                cp.start()
                cps.append(cp)
        for cp in cps:
            cp.wait()

    out_shape = tuple(jax.ShapeDtypeStruct((3,) + s.shape[1:], s.dtype) for s in parts)
    return pl.pallas_call(
        body, out_shape=out_shape, in_specs=_hbm_specs(n), out_specs=tuple(_hbm_specs(n)),
        scratch_shapes=[pltpu.SemaphoreType.DMA((n, 3)), pltpu.SemaphoreType.DMA((n, 3))],
        name="chip_exchange", compiler_params=pltpu.CompilerParams(has_side_effects=True))(*parts)


def sum_slots_into(parts, recv, chip_idx, acc, layer, depth):
    _, H, C = parts.shape
    tr = _pick(H, tuple(t for t in (1024, 512, 256, 128, 64, 32, 16) if t * C * 4 <= (1 << 20)))
    has_acc = acc is not None

    def body(chip_ref, p_ref, r_ref, *rest):
        o_ref = rest[-1]
        o_ref[...] = ((r_ref[0].astype(F32) + r_ref[1].astype(F32)) + r_ref[2].astype(F32)) + p_ref[...].astype(F32)

    in_specs = [pl.BlockSpec((None, tr, C), lambda i, chip_ref: (chip_ref[0], i, 0)),
                pl.BlockSpec((3, tr, C), lambda i, chip_ref: (0, i, 0))]
    args = [chip_idx, parts, recv]
    if has_acc:
        in_specs.append(pl.BlockSpec(memory_space=pl.ANY))
        args.append(acc)
    grid_spec = pltpu.PrefetchScalarGridSpec(
        num_scalar_prefetch=1, grid=(H // tr,), in_specs=in_specs,
        out_specs=pl.BlockSpec((None, tr, C), lambda i, chip_ref: (layer, i, 0)))
    return pl.pallas_call(body, out_shape=jax.ShapeDtypeStruct((depth, H, C), F32), grid_spec=grid_spec,
                          input_output_aliases=({3: 0} if has_acc else {}), name="sum_slots",
                          compiler_params=_params(("parallel",)))(*args)


def sibling_swap(mine):
    n = len(mine)

    def body(*refs):
        src, out = refs[:n], refs[n:2 * n]
        send_sems, recv_sems = refs[2 * n:]
        x, y, c, _ = _place()
        cps = []
        for t in range(n):
            cp = pltpu.make_async_remote_copy(
                src_ref=src[t], dst_ref=out[t], send_sem=send_sems.at[t], recv_sem=recv_sems.at[t],
                device_id=(x, y, 1 - c), device_id_type=MESH)
            cp.start()
            cps.append(cp)
        for cp in cps:
            cp.wait()

    out_shape = tuple(jax.ShapeDtypeStruct(s.shape, s.dtype) for s in mine)
    return pl.pallas_call(
        body, out_shape=out_shape, in_specs=_hbm_specs(n), out_specs=tuple(_hbm_specs(n)),
        scratch_shapes=[pltpu.SemaphoreType.DMA((n,)), pltpu.SemaphoreType.DMA((n,))],
        name="sibling_swap", compiler_params=pltpu.CompilerParams(has_side_effects=True))(*mine)


def reduce_layer(grads, c_idx, chip_idx, accs, layer, depth):
    recv = sibling_exchange_halves(grads)
    parts = [pair_sum(g, r, c_idx) for g, r in zip(grads, recv)]
    slots = chip_exchange(parts)
    return [sum_slots_into(p, s, chip_idx, a, layer, depth) for p, s, a in zip(parts, slots, accs)]


def adamw_halves(w, mine, theirs, m, v, c_idx):
    L, R, C = w.shape
    half = R // 2
    tr = _pick(half, tuple(t for t in (2048, 1024, 512, 256, 128, 64, 32, 16, 8) if t * C * 4 <= (1 << 20)))
    nb = half // tr
    c1 = 1.0 / (1.0 - ADAM_B1 ** ADAM_STEP)
    c2 = 1.0 / (1.0 - ADAM_B2 ** ADAM_STEP)

    def body(c_ref, w_ref, a_ref, b_ref, m_ref, v_ref, g_ref, d_ref, nm_ref, nv_ref):
        gv = jnp.where(c_ref[0] == pl.program_id(1), a_ref[...], b_ref[...])
        nm = ADAM_B1 * m_ref[...] + (1.0 - ADAM_B1) * gv
        nv = ADAM_B2 * v_ref[...] + (1.0 - ADAM_B2) * (gv * gv)
        g_ref[...] = gv
        d_ref[...] = -ADAM_LR * ((nm * c1) / (jnp.sqrt(nv * c2) + ADAM_EPS) + ADAM_WD * w_ref[...])
        nm_ref[...] = nm
        nv_ref[...] = nv

    full = pl.BlockSpec((None, tr, C), lambda l, h, i, c_ref: (l, h * nb + i, 0))
    part = pl.BlockSpec((None, tr, C), lambda l, h, i, c_ref: (l, i, 0))
    grid_spec = pltpu.PrefetchScalarGridSpec(
        num_scalar_prefetch=1, grid=(L, 2, nb), in_specs=[full, part, part, full, full],
        out_specs=(full, full, full, full))
    out = jax.ShapeDtypeStruct((L, R, C), F32)
    return pl.pallas_call(body, out_shape=(out, out, out, out), grid_spec=grid_spec, name="adamw_halves",
                          compiler_params=_params(("parallel", "parallel", "parallel")))(c_idx, w, mine, theirs, m, v)


BIG = ("w_in", "mla_w_qb", "mla_w_kvb", "w_out", "xa_wq", "xa_wk", "xa_wv", "xa_wo", "ffn_w_up", "ffn_w_down")
COL_SHARDED = ("w_in", "mla_w_qb", "mla_w_kvb", "ffn_w_up")
SMALL_REPL = ("norm_mix", "dn_a_log", "dn_dt_bias", "dn_out_norm", "mla_q_norm", "mla_kv_norm", "mem_norm",
              "norm_xattn", "norm_ffn", "ffn_conv_bias", "norm_final")
SMALL_SHARDED = ("dn_conv", "ffn_conv")
WEIGHTS = ("norm_mix", "w_in", "dn_conv", "dn_a_log", "dn_dt_bias", "dn_out_norm", "mla_q_norm", "mla_w_qb",
           "mla_kv_norm", "mla_w_kvb", "w_out", "mem_norm", "norm_xattn", "xa_wq", "xa_wk", "xa_wv", "xa_wo",
           "norm_ffn", "ffn_w_up", "ffn_conv", "ffn_conv_bias", "ffn_w_down", "norm_final")


def _pad_cols(a, cp):
    c = a.shape[-1]
    if c == cp:
        return a
    return jnp.pad(a, [(0, 0)] * (a.ndim - 1) + [(0, cp - c)])


def _pack(arrs):
    flat = jnp.concatenate([a.reshape(-1).astype(F32) for a in arrs])
    rows = -(-flat.shape[0] // LANES)
    rows = -(-rows // 8) * 8
    return jnp.pad(flat, (0, rows * LANES - flat.shape[0])).reshape(rows, LANES)


def _unpack(buf, like):
    flat = buf.reshape(-1)
    out, off = [], 0
    for a in like:
        n = int(np.prod(a.shape))
        out.append(flat[off:off + n].reshape(a.shape))
        off += n
    return out


def _heads(t, h):
    s = t.shape[0]
    return jnp.transpose(t.reshape(s, h, t.shape[1] // h), (1, 0, 2))


def _unheads(t):
    h, s, d = t.shape
    return jnp.transpose(t, (1, 0, 2)).reshape(s, h * d)


def _rope(t, cos, sin):
    t1, t2 = jnp.split(t, 2, axis=-1)
    return jnp.concatenate([t1 * cos - t2 * sin, t2 * cos + t1 * sin], axis=-1)


def _forward_loss(p, x, mem, cos, sin, target, dims):
    S, D = x.shape
    depth, dn_h, mla_h, d_ff, c_in, c_in_pad, ffn_tn = dims
    dn_w = dn_h * DN_HEAD_DIM
    n_chunks = S // CHUNK
    mem_n = rms_norm(mem, p["mem_norm"], EPS)
    h = x
    ffn_blocks = (2 * d_ff) // ffn_tn
    half_blocks = ffn_blocks // 2

    def ffn_perm(nb):
        return jnp.where(nb < half_blocks, 2 * nb, 2 * (nb - half_blocks) + 1)

    def ffn_layout(a):
        k = a.shape[0]
        return jnp.transpose(a.reshape(k, 2, half_blocks, ffn_tn), (0, 2, 1, 3)).reshape(k, 2 * d_ff)

    for l in range(depth):
        u = rms_norm(h, p["norm_mix"][l], EPS)
        projp = linear(u, p["w_in"][l])
        proj = jnp.concatenate([projp[:, g * c_in_pad:g * c_in_pad + c_in] for g in range(N_CHIPS)], axis=1)
        o0 = 3 * dn_w
        dz = proj[:, o0:o0 + dn_w]
        db = proj[:, o0 + dn_w:o0 + dn_w + dn_h]
        da = proj[:, o0 + dn_w + dn_h:o0 + dn_w + 2 * dn_h]
        o1 = o0 + dn_w + 2 * dn_h
        mq = proj[:, o1:o1 + MLA_Q_RANK]
        mkv = proj[:, o1 + MLA_Q_RANK:]
        qkv = conv_silu(proj[:, :o0], p["dn_conv"][l])
        qd = l2_norm(qkv[:, :dn_w].reshape(S * dn_h, DN_HEAD_DIM)).reshape(S, dn_w)
        kd = l2_norm(qkv[:, dn_w:2 * dn_w].reshape(S * dn_h, DN_HEAD_DIM)).reshape(S, dn_w)
        vd = qkv[:, 2 * dn_w:]
        beta = jax.nn.sigmoid(db)
        g = -jnp.exp(p["dn_a_log"][l]) * jax.nn.softplus(da + p["dn_dt_bias"][l])
        g_t, beta_t = g.T, beta.T
        o_dn = delta_rule(qd, kd, vd, g_t[:, :, None], g_t.reshape(dn_h, n_chunks, 1, CHUNK), beta_t[:, :, None])
        o_dn = rms_norm(o_dn.reshape(S * dn_h, DN_HEAD_DIM), p["dn_out_norm"][l], EPS).reshape(S, dn_w)
        o_dn = o_dn * jax.nn.silu(dz)
        qf = linear(rms_norm(mq, p["mla_q_norm"][l], EPS), p["mla_w_qb"][l]).reshape(S, mla_h, MLA_NOPE + MLA_ROPE)
        q_pe = _rope(qf[..., MLA_NOPE:], cos[:, None, :], sin[:, None, :])
        k_pe = _rope(mkv[:, MLA_KV_RANK:], cos, sin)
        kv = linear(rms_norm(mkv[:, :MLA_KV_RANK], p["mla_kv_norm"][l], EPS), p["mla_w_kvb"][l])
        kv = kv.reshape(S, mla_h, MLA_NOPE + MLA_V)
        qa = jnp.transpose(jnp.concatenate([qf[..., :MLA_NOPE], q_pe], axis=-1), (1, 0, 2))
        ka = jnp.transpose(jnp.concatenate(
            [kv[..., :MLA_NOPE], jnp.broadcast_to(k_pe[:, None, :], (S, mla_h, MLA_ROPE))], axis=-1), (1, 0, 2))
        va = jnp.transpose(kv[..., MLA_NOPE:], (1, 0, 2))
        o_mla = _unheads(attention(qa, ka, va, (MLA_NOPE + MLA_ROPE) ** -0.5, True, None))
        h = linear(jnp.concatenate([o_dn, o_mla], axis=-1), p["w_out"][l], res=h)
        hn = rms_norm(h, p["norm_xattn"][l], EPS)
        xo = attention(linear(hn, p["xa_wq"][l]), linear(mem_n, p["xa_wk"][l]), linear(mem_n, p["xa_wv"][l]),
                       (D // XA_HEADS) ** -0.5, False, XA_HEADS)
        h = linear(xo, p["xa_wo"][l], res=h)
        hn = rms_norm(h, p["norm_ffn"][l], EPS)
        pre = linear(hn, p["ffn_w_up"][l], tn=ffn_tn, colperm=ffn_perm)
        act = conv_glu(pre, ffn_layout(p["ffn_conv"][l]), ffn_layout(p["ffn_conv_bias"][l][None, :]), ffn_tn)
        h = linear(act, p["ffn_w_down"][l], res=h)
    y = rms_norm(h, p["norm_final"], EPS)
    return sq_loss(y, target)


def kernel(x, mem, positions, norm_mix, w_in, dn_conv, dn_a_log, dn_dt_bias, dn_out_norm, mla_q_norm, mla_w_qb, mla_kv_norm, mla_w_kvb, w_out, mem_norm, norm_xattn, xa_wq, xa_wk, xa_wv, xa_wo, norm_ffn, ffn_w_up, ffn_conv, ffn_conv_bias, ffn_w_down, norm_final, loss_target, m_norm_mix, m_w_in, m_dn_conv, m_dn_a_log, m_dn_dt_bias, m_dn_out_norm, m_mla_q_norm, m_mla_w_qb, m_mla_kv_norm, m_mla_w_kvb, m_w_out, m_mem_norm, m_norm_xattn, m_xa_wq, m_xa_wk, m_xa_wv, m_xa_wo, m_norm_ffn, m_ffn_w_up, m_ffn_conv, m_ffn_conv_bias, m_ffn_w_down, m_norm_final, v_norm_mix, v_w_in, v_dn_conv, v_dn_a_log, v_dn_dt_bias, v_dn_out_norm, v_mla_q_norm, v_mla_w_qb, v_mla_kv_norm, v_mla_w_kvb, v_w_out, v_mem_norm, v_norm_xattn, v_xa_wq, v_xa_wk, v_xa_wv, v_xa_wo, v_norm_ffn, v_ffn_w_up, v_ffn_conv, v_ffn_conv_bias, v_ffn_w_down, v_norm_final):
    w = dict(norm_mix=norm_mix, w_in=w_in, dn_conv=dn_conv, dn_a_log=dn_a_log, dn_dt_bias=dn_dt_bias, dn_out_norm=dn_out_norm, mla_q_norm=mla_q_norm, mla_w_qb=mla_w_qb, mla_kv_norm=mla_kv_norm, mla_w_kvb=mla_w_kvb, w_out=w_out, mem_norm=mem_norm, norm_xattn=norm_xattn, xa_wq=xa_wq, xa_wk=xa_wk, xa_wv=xa_wv, xa_wo=xa_wo, norm_ffn=norm_ffn, ffn_w_up=ffn_w_up, ffn_conv=ffn_conv, ffn_conv_bias=ffn_conv_bias, ffn_w_down=ffn_w_down, norm_final=norm_final)
    m = dict(norm_mix=m_norm_mix, w_in=m_w_in, dn_conv=m_dn_conv, dn_a_log=m_dn_a_log, dn_dt_bias=m_dn_dt_bias, dn_out_norm=m_dn_out_norm, mla_q_norm=m_mla_q_norm, mla_w_qb=m_mla_w_qb, mla_kv_norm=m_mla_kv_norm, mla_w_kvb=m_mla_w_kvb, w_out=m_w_out, mem_norm=m_mem_norm, norm_xattn=m_norm_xattn, xa_wq=m_xa_wq, xa_wk=m_xa_wk, xa_wv=m_xa_wv, xa_wo=m_xa_wo, norm_ffn=m_norm_ffn, ffn_w_up=m_ffn_w_up, ffn_conv=m_ffn_conv, ffn_conv_bias=m_ffn_conv_bias, ffn_w_down=m_ffn_w_down, norm_final=m_norm_final)
    v = dict(norm_mix=v_norm_mix, w_in=v_w_in, dn_conv=v_dn_conv, dn_a_log=v_dn_a_log, dn_dt_bias=v_dn_dt_bias, dn_out_norm=v_dn_out_norm, mla_q_norm=v_mla_q_norm, mla_w_qb=v_mla_w_qb, mla_kv_norm=v_mla_kv_norm, mla_w_kvb=v_mla_w_kvb, w_out=v_w_out, mem_norm=v_mem_norm, norm_xattn=v_norm_xattn, xa_wq=v_xa_wq, xa_wk=v_xa_wk, xa_wv=v_xa_wv, xa_wo=v_xa_wo, norm_ffn=v_norm_ffn, ffn_w_up=v_ffn_w_up, ffn_conv=v_ffn_conv, ffn_conv_bias=v_ffn_conv_bias, ffn_w_down=v_ffn_w_down, norm_final=v_norm_final)

    S, D = x.shape[1], x.shape[2]
    depth = w_in.shape[0]
    dn_h = (D // 2) // DN_HEAD_DIM
    mla_h = (D - dn_h * DN_HEAD_DIM) // MLA_V
    d_ff = ffn_w_down.shape[1] * N_CHIPS
    c_in = w_in.shape[2]
    c_in_pad = -(-c_in // LANES) * LANES
    ffn_tn = _pick(ffn_w_up.shape[2], (256, 128))
    dims = (depth, dn_h, mla_h, d_ff, c_in, c_in_pad, ffn_tn)
    chip = 2 * lax.axis_index("x") + lax.axis_index("y")
    c_idx = lax.axis_index("c").astype(jnp.int32).reshape(1)

    gathered = {n: [] for n in BIG}
    for l in range(depth):
        shards = []
        for n in BIG:
            s = w[n][l].astype(BF16)
            if n == "w_in":
                s = _pad_cols(s, c_in_pad)
            shards.append(s)
        for n, own, full in zip(BIG, shards, gather_weights(shards)):
            full = lax.dynamic_update_slice(full, own[None], (chip, 0, 0))
            if n not in COL_SHARDED:
                full = full.reshape(1, N_CHIPS * full.shape[1], full.shape[2])
            gathered[n].append(full)
    small_sh = small_allgather(_pack([w[n] for n in SMALL_SHARDED]))
    params = dict(gathered)
    for n, parts in zip(SMALL_SHARDED, zip(*[_unpack(small_sh[2 * j], [w[k] for k in SMALL_SHARDED])
                                              for j in range(N_CHIPS)])):
        params[n] = jnp.concatenate(parts, axis=-1)
    for n in SMALL_REPL:
        params[n] = w[n]

    inv = ROPE_BASE ** (-jnp.arange(0, MLA_ROPE, 2, dtype=F32) / MLA_ROPE)
    ang = positions[0].astype(F32)[:, None] * inv
    cos, sin = jnp.cos(ang), jnp.sin(ang)
    loss_fn = lambda p, xx: _forward_loss(p, xx, mem[0], cos, sin, loss_target[0], dims)
    loss_local, (gp, gx) = jax.value_and_grad(loss_fn, argnums=(0, 1))(params, x[0])

    small_names = SMALL_REPL + SMALL_SHARDED
    packed = _pack([loss_local.reshape(1)] + [gp[n] for n in small_names])
    summed = sum_blocks(small_allgather(packed))
    unpacked = _unpack(summed, [loss_local.reshape(1)] + [gp[n] for n in small_names])
    loss = unpacked[0][0]
    grads = dict(zip(small_names, unpacked[1:]))
    for n in SMALL_SHARDED:
        cs = w[n].shape[-1]
        grads[n] = lax.dynamic_slice_in_dim(grads[n], chip * cs, cs, axis=-1)

    chip_idx = chip.astype(jnp.int32).reshape(1)
    accs = [None] * len(BIG)
    for l in range(depth):
        gl = []
        for n in BIG:
            g = gp[n][l]
            if n not in COL_SHARDED:
                g = g.reshape(N_CHIPS, g.shape[1] // N_CHIPS, g.shape[2])
            gl.append(g)
        accs = reduce_layer(gl, c_idx, chip_idx, accs, l, depth)
    theirs = sibling_swap(accs)

    delta, new_m, new_v = {}, {}, {}
    for n, mine, other in zip(BIG, accs, theirs):
        if n == "w_in":
            lo, hi = jnp.where(c_idx[0] == 0, mine, other), jnp.where(c_idx[0] == 0, other, mine)
            grads[n] = jnp.concatenate([lo, hi], axis=1)[:, :, :c_in]
            delta[n], new_m[n], new_v[n] = adamw(w[n], grads[n], m[n], v[n])
        else:
            grads[n], delta[n], new_m[n], new_v[n] = adamw_halves(w[n], mine, other, m[n], v[n], c_idx)
    sw, sg, sm, sv = (_pack([d[n] for n in small_names]) for d in (w, grads, m, v))
    sd, snm, snv = adamw(sw, sg, sm, sv)
    like = [w[n] for n in small_names]
    for d, buf in ((delta, sd), (new_m, snm), (new_v, snv)):
        for n, a in zip(small_names, _unpack(buf, like)):
            d[n] = a

    return (loss, gx[None], *[grads[n] for n in WEIGHTS], *[delta[n] for n in WEIGHTS],
            *[new_m[n] for n in WEIGHTS], *[new_v[n] for n in WEIGHTS])
```

```python
import functools

import numpy as np
import jax
import jax.numpy as jnp
from jax import lax
from jax.experimental import pallas as pl
from jax.experimental.pallas import tpu as pltpu

F32 = jnp.float32
BF16 = jnp.bfloat16
MESH = pl.DeviceIdType.MESH
HIGHEST = lax.Precision.HIGHEST

LANES = 128
VMEM_LIMIT = 56 * 1024 * 1024
N_CHIPS = 4
N_DEV = 8

CHUNK = 64
DN_HEAD_DIM = 128
MLA_NOPE, MLA_ROPE, MLA_V = 128, 64, 128
MLA_Q_RANK, MLA_KV_RANK = 512, 256
XA_HEADS = 4
ROPE_BASE = 10000.0
EPS = 1e-6

ADAM_LR, ADAM_B1, ADAM_B2, ADAM_EPS, ADAM_WD, ADAM_STEP = 0.001, 0.9, 0.999, 1e-08, 0.01, 10

NN = (((1,), (0,)), ((), ()))
NT = (((1,), (1,)), ((), ()))
TN = (((0,), (0,)), ((), ()))


def _pick(dim, cands):
    for c in cands:
        if c <= dim and dim % c == 0:
            return c
    return dim


def _params(sem=None):
    return pltpu.CompilerParams(dimension_semantics=sem, vmem_limit_bytes=VMEM_LIMIT)


def _mm_call(name, a, b, res, dims, grid, a_spec, b_spec, o_spec, out_shape, acc_shape):
    nk = grid[3]
    has_res = res is not None

    def body(*refs):
        if has_res:
            a_ref, b_ref, r_ref, o_ref, acc = refs
        else:
            a_ref, b_ref, o_ref, acc = refs
        kk = pl.program_id(3)

        @pl.when(kk == 0)
        def _():
            acc[...] = jnp.zeros_like(acc)

        acc[...] += lax.dot_general(a_ref[...].astype(BF16), b_ref[...].astype(BF16), dims,
                                    preferred_element_type=F32)

        @pl.when(kk == nk - 1)
        def _():
            r = acc[...]
            if has_res:
                r = r + r_ref[...]
            o_ref[...] = r.astype(o_ref.dtype)

    in_specs = [a_spec, b_spec] + ([o_spec] if has_res else [])
    args = (a, b) + ((res,) if has_res else ())
    return pl.pallas_call(
        body, out_shape=out_shape, grid=grid, in_specs=in_specs, out_specs=o_spec,
        scratch_shapes=[pltpu.VMEM(acc_shape, F32)], name=name,
        compiler_params=_params(("parallel", "parallel", "parallel", "arbitrary")),
    )(*args)


_TM = (1024, 512, 256, 128, 64, 32, 16, 8)
_TN = (1024, 768, 640, 512, 384, 256, 128)


def _ident(nb):
    return nb


def mm_nn_raw(a, w, res=None, tn=None, colperm=_ident, name="mm_nn"):
    M, K = a.shape
    G, _, C = w.shape
    tm = _pick(M, _TM)
    tn = tn or _pick(C, _TN)
    tk = K if K <= 2048 else _pick(K, (1408, 1024, 512, 256, 128))
    nj = C // tn
    grid = (G, M // tm, nj, K // tk)
    return _mm_call(
        name, a, w, res, NN, grid,
        pl.BlockSpec((tm, tk), lambda g, i, j, k: (i, k)),
        pl.BlockSpec((None, tk, tn), lambda g, i, j, k: (g, k, j)),
        pl.BlockSpec((tm, tn), lambda g, i, j, k: (i, colperm(g * nj + j))),
        jax.ShapeDtypeStruct((M, G * C), F32), (tm, tn))


def mm_nt_raw(d, w, tn=None, colperm=_ident, name="mm_nt"):
    M = d.shape[0]
    G, K, C = w.shape
    tm = _pick(M, _TM)
    tko = _pick(K, _TN)
    tc = tn or _pick(C, (2048, 1408, 1280, 1024, 512, 384, 256, 128))
    ncb = C // tc
    grid = (1, M // tm, K // tko, G * ncb)
    return _mm_call(
        name, d, w, None, NT, grid,
        pl.BlockSpec((tm, tc), lambda g, i, j, k: (i, colperm(k))),
        pl.BlockSpec((None, tko, tc), lambda g, i, j, k: (k // ncb, j, k % ncb)),
        pl.BlockSpec((tm, tko), lambda g, i, j, k: (i, j)),
        jax.ShapeDtypeStruct((M, K), F32), (tm, tko))


def mm_tn_raw(at, d, G, tn=None, colperm=_ident, name="mm_tn"):
    K, M = at.shape
    C = d.shape[1] // G
    tko = _pick(K, _TM)
    tn = tn or _pick(C, _TN)
    tm = M if M <= 2048 else _pick(M, (2048, 1024, 512, 256, 128))
    nj = C // tn
    grid = (G, K // tko, nj, M // tm)
    return _mm_call(
        name, at, d, None, NN, grid,
        pl.BlockSpec((tko, tm), lambda g, i, j, k: (i, k)),
        pl.BlockSpec((tm, tn), lambda g, i, j, k: (k, colperm(g * nj + j))),
        pl.BlockSpec((None, tko, tn), lambda g, i, j, k: (g, i, j)),
        jax.ShapeDtypeStruct((G, K, C), BF16), (tko, tn))


def _make_linear(tn, colperm, with_res):
    @jax.custom_vjp
    def lin(a, w, res):
        return mm_nn_raw(a.astype(BF16), w, res if with_res else None, tn=tn, colperm=colperm)

    def fwd(a, w, res):
        a16 = a.astype(BF16)
        return mm_nn_raw(a16, w, res if with_res else None, tn=tn, colperm=colperm), (a16, w)

    def bwd(saved, dout):
        a16, w = saved
        d16 = dout.astype(BF16)
        da = mm_nt_raw(d16, w, tn=tn, colperm=colperm)
        dw = mm_tn_raw(a16.T, d16, w.shape[0], tn=tn, colperm=colperm)
        return da, dw, (dout if with_res else None)

    lin.defvjp(fwd, bwd)
    return lin


def linear(a, w, res=None, tn=None, colperm=_ident):
    if res is None:
        return _make_linear(tn, colperm, False)(a, w, None)
    return _make_linear(tn, colperm, True)(a, w, res)


def _row_tile(R, D):
    return _pick(R, tuple(t for t in (2048, 1024, 512, 256, 128, 64, 32, 16, 8) if t * D * 4 <= (2 << 20)))


def _norm_fwd_call(x, gain, eps, inv_n):
    R, D = x.shape
    tr = _row_tile(R, D)
    has_gain = gain is not None

    def body(*refs):
        if has_gain:
            x_ref, g_ref, y_ref = refs
        else:
            x_ref, y_ref = refs
        xv = x_ref[...]
        r = lax.rsqrt(jnp.sum(xv * xv, axis=-1, keepdims=True) * inv_n + eps)
        y = xv * r
        if has_gain:
            y = y * g_ref[...]
        y_ref[...] = y

    row = pl.BlockSpec((tr, D), lambda i: (i, 0))
    in_specs = [row] + ([pl.BlockSpec((1, D), lambda i: (0, 0))] if has_gain else [])
    args = (x,) + ((gain.reshape(1, D),) if has_gain else ())
    return pl.pallas_call(body, out_shape=jax.ShapeDtypeStruct((R, D), F32), grid=(R // tr,),
                          in_specs=in_specs, out_specs=row, name="norm_fwd",
                          compiler_params=_params(("parallel",)))(*args)


def _norm_bwd_call(x, gain, dy, eps, inv_n):
    R, D = x.shape
    tr = _row_tile(R, D)
    has_gain = gain is not None

    def body(*refs):
        if has_gain:
            x_ref, g_ref, dy_ref, dx_ref, dg_ref = refs
        else:
            x_ref, dy_ref, dx_ref = refs
        xv = x_ref[...]
        dyv = dy_ref[...]
        r = lax.rsqrt(jnp.sum(xv * xv, axis=-1, keepdims=True) * inv_n + eps)
        xh = xv * r
        dxh = dyv * g_ref[...] if has_gain else dyv
        dx_ref[...] = r * (dxh - xh * (inv_n * jnp.sum(dxh * xh, axis=-1, keepdims=True)))
        if has_gain:
            @pl.when(pl.program_id(0) == 0)
            def _():
                dg_ref[...] = jnp.zeros_like(dg_ref)

            dg_ref[...] += jnp.sum(dyv * xh, axis=0, keepdims=True)

    row = pl.BlockSpec((tr, D), lambda i: (i, 0))
    vec = pl.BlockSpec((1, D), lambda i: (0, 0))
    if has_gain:
        dx, dg = pl.pallas_call(
            body, out_shape=(jax.ShapeDtypeStruct((R, D), F32), jax.ShapeDtypeStruct((1, D), F32)),
            grid=(R // tr,), in_specs=[row, vec, row], out_specs=(row, vec), name="norm_bwd",
            compiler_params=_params(("arbitrary",)))(x, gain.reshape(1, D), dy)
        return dx, dg.reshape(D)
    dx = pl.pallas_call(body, out_shape=jax.ShapeDtypeStruct((R, D), F32), grid=(R // tr,),
                        in_specs=[row, row], out_specs=row, name="l2norm_bwd",
                        compiler_params=_params(("parallel",)))(x, dy)
    return dx, None


@functools.partial(jax.custom_vjp, nondiff_argnums=(2,))
def rms_norm(x, gain, eps):
    return _norm_fwd_call(x, gain, eps, 1.0 / x.shape[-1])


def _rms_norm_fwd(x, gain, eps):
    return rms_norm(x, gain, eps), (x, gain)


def _rms_norm_bwd(eps, saved, dy):
    x, gain = saved
    return _norm_bwd_call(x, gain, dy, eps, 1.0 / x.shape[-1])


rms_norm.defvjp(_rms_norm_fwd, _rms_norm_bwd)


@jax.custom_vjp
def l2_norm(x):
    return _norm_fwd_call(x, None, EPS, 1.0)


def _l2_norm_fwd(x):
    return l2_norm(x), x


def _l2_norm_bwd(x, dy):
    return (_norm_bwd_call(x, None, dy, EPS, 1.0)[0],)


l2_norm.defvjp(_l2_norm_fwd, _l2_norm_bwd)


def _group_norm_call(x, gain, dy, groups, eps, mean):
    R, D = x.shape
    gs = D // groups
    tr = _row_tile(R, D)
    inv_n = 1.0 / gs if mean else 1.0
    has_gain = gain is not None
    backward = dy is not None

    def body(*refs):
        refs = list(refs)
        x_ref = refs.pop(0)
        g_ref = refs.pop(0) if has_gain else None
        dy_ref = refs.pop(0) if backward else None
        out_ref = refs.pop(0)
        if backward and has_gain:
            dg_ref = refs.pop(0)

            @pl.when(pl.program_id(0) == 0)
            def _():
                dg_ref[...] = jnp.zeros_like(dg_ref)

        for gi in range(groups):
            cols = slice(gi * gs, (gi + 1) * gs)
            xv = x_ref[:, cols]
            r = lax.rsqrt(jnp.sum(xv * xv, axis=-1, keepdims=True) * inv_n + eps)
            xh = xv * r
            if not backward:
                out_ref[:, cols] = xh * g_ref[...] if has_gain else xh
                continue
            dyv = dy_ref[:, cols]
            dxh = dyv * g_ref[...] if has_gain else dyv
            out_ref[:, cols] = r * (dxh - xh * (inv_n * jnp.sum(dxh * xh, axis=-1, keepdims=True)))
            if has_gain:
                dg_ref[...] += jnp.sum(dyv * xh, axis=0, keepdims=True)

    row = pl.BlockSpec((tr, D), lambda i: (i, 0))
    vec = pl.BlockSpec((1, gs), lambda i: (0, 0))
    in_specs = [row] + ([vec] if has_gain else []) + ([row] if backward else [])
    args = (x,) + ((gain.reshape(1, gs),) if has_gain else ()) + ((dy,) if backward else ())
    full = jax.ShapeDtypeStruct((R, D), F32)
    if backward and has_gain:
        dx, dg = pl.pallas_call(body, out_shape=(full, jax.ShapeDtypeStruct((1, gs), F32)), grid=(R // tr,),
                                in_specs=in_specs, out_specs=(row, vec), name="group_norm_bwd",
                                compiler_params=_params(("arbitrary",)))(*args)
        return dx, dg.reshape(gs)
    out = pl.pallas_call(body, out_shape=full, grid=(R // tr,), in_specs=in_specs, out_specs=row,
                         name="group_norm_bwd" if backward else "group_norm_fwd",
                         compiler_params=_params(("parallel",)))(*args)
    return (out, None) if backward else out


@functools.partial(jax.custom_vjp, nondiff_argnums=(2, 3))
def group_rms_norm(x, gain, groups, eps):
    return _group_norm_call(x, gain, None, groups, eps, True)


def _group_rms_norm_fwd(x, gain, groups, eps):
    return group_rms_norm(x, gain, groups, eps), (x, gain)


def _group_rms_norm_bwd(groups, eps, saved, dy):
    x, gain = saved
    return _group_norm_call(x, gain, dy, groups, eps, True)


group_rms_norm.defvjp(_group_rms_norm_fwd, _group_rms_norm_bwd)


@functools.partial(jax.custom_vjp, nondiff_argnums=(1,))
def group_l2_norm(x, groups):
    return _group_norm_call(x, None, None, groups, EPS, False)


def _group_l2_norm_fwd(x, groups):
    return group_l2_norm(x, groups), x


def _group_l2_norm_bwd(groups, x, dy):
    return (_group_norm_call(x, None, dy, groups, EPS, False)[0],)


group_l2_norm.defvjp(_group_l2_norm_fwd, _group_l2_norm_bwd)


def _attn_mask(tq, sk):
    q0 = pl.program_id(1) * tq
    qc = (q0 + lax.broadcasted_iota(jnp.int32, (tq, sk), 0)) // CHUNK
    kc = lax.broadcasted_iota(jnp.int32, (tq, sk), 1) // CHUNK
    return kc <= qc


def _attn_dims(q, k, v, heads):
    if heads is None:
        H, S, dk = q.shape
        return H, S, dk, k.shape[1], v.shape[2]
    return heads, q.shape[0], q.shape[1] // heads, k.shape[0], v.shape[1] // heads


def _attn_specs(heads, tq, dk, Sk, dv):
    if heads is None:
        return (pl.BlockSpec((None, tq, dk), lambda h, i: (h, i, 0)), pl.BlockSpec((None, Sk, dk), lambda h, i: (h, 0, 0)),
                pl.BlockSpec((None, Sk, dv), lambda h, i: (h, 0, 0)), pl.BlockSpec((None, tq, dv), lambda h, i: (h, i, 0)))
    return (pl.BlockSpec((tq, dk), lambda h, i: (i, h)), pl.BlockSpec((Sk, dk), lambda h, i: (0, h)),
            pl.BlockSpec((Sk, dv), lambda h, i: (0, h)), pl.BlockSpec((tq, dv), lambda h, i: (i, h)))


def _attn_fwd_call(q, k, v, scale, causal, heads):
    H, S, dk, Sk, dv = _attn_dims(q, k, v, heads)
    tq = _pick(S, (256, 128, 64))
    qs, ks, vs, os_ = _attn_specs(heads, tq, dk, Sk, dv)

    def body(q_ref, k_ref, v_ref, o_ref, lse_ref):
        s = lax.dot_general(q_ref[...].astype(BF16), k_ref[...].astype(BF16), NT,
                            preferred_element_type=F32) * scale
        if causal:
            s = jnp.where(_attn_mask(tq, Sk), s, -1e30)
        m = jnp.max(s, axis=-1, keepdims=True)
        p = jnp.exp(s - m)
        l = jnp.sum(p, axis=-1, keepdims=True)
        o = lax.dot_general(p.astype(BF16), v_ref[...].astype(BF16), NN, preferred_element_type=F32)
        o_ref[...] = o / l
        lse_ref[...] = m + jnp.log(l)

    o_shape = (H, S, dv) if heads is None else (S, H * dv)
    return pl.pallas_call(
        body,
        out_shape=(jax.ShapeDtypeStruct(o_shape, F32), jax.ShapeDtypeStruct((H, S, 1), F32)),
        grid=(H, S // tq), in_specs=[qs, ks, vs],
        out_specs=(os_, pl.BlockSpec((None, tq, 1), lambda h, i: (h, i, 0))),
        name="attn_fwd", compiler_params=_params(("parallel", "parallel")))(q, k, v)


def _attn_bwd_call(q, k, v, o, lse, do, scale, causal, heads):
    H, S, dk, Sk, dv = _attn_dims(q, k, v, heads)
    tq = _pick(S, (256, 128, 64))
    qs, ks, vs, os_ = _attn_specs(heads, tq, dk, Sk, dv)

    def body(q_ref, k_ref, v_ref, o_ref, lse_ref, do_ref, dq_ref, dk_ref, dv_ref):
        qb = q_ref[...].astype(BF16)
        kb = k_ref[...].astype(BF16)
        dob = do_ref[...].astype(BF16)
        s = lax.dot_general(qb, kb, NT, preferred_element_type=F32) * scale
        if causal:
            s = jnp.where(_attn_mask(tq, Sk), s, -1e30)
        p = jnp.exp(s - lse_ref[...])
        dp = lax.dot_general(dob, v_ref[...].astype(BF16), NT, preferred_element_type=F32)
        delta = jnp.sum(do_ref[...] * o_ref[...], axis=-1, keepdims=True)
        ds = (p * (dp - delta) * scale).astype(BF16)
        dq_ref[...] = lax.dot_general(ds, kb, NN, preferred_element_type=F32)

        @pl.when(pl.program_id(1) == 0)
        def _():
            dk_ref[...] = jnp.zeros_like(dk_ref)
            dv_ref[...] = jnp.zeros_like(dv_ref)

        dk_ref[...] += lax.dot_general(ds, qb, TN, preferred_element_type=F32)
        dv_ref[...] += lax.dot_general(p.astype(BF16), dob, TN, preferred_element_type=F32)

    ls = pl.BlockSpec((None, tq, 1), lambda h, i: (h, i, 0))
    return pl.pallas_call(
        body,
        out_shape=(jax.ShapeDtypeStruct(q.shape, F32), jax.ShapeDtypeStruct(k.shape, F32),
                   jax.ShapeDtypeStruct(v.shape, F32)),
        grid=(H, S // tq), in_specs=[qs, ks, vs, os_, ls, os_], out_specs=(qs, ks, vs),
        name="attn_bwd", compiler_params=_params(("parallel", "arbitrary")))(q, k, v, o, lse, do)


@functools.partial(jax.custom_vjp, nondiff_argnums=(3, 4, 5))
def attention(q, k, v, scale, causal, heads=None):
    return _attn_fwd_call(q, k, v, scale, causal, heads)[0]


def _attention_fwd(q, k, v, scale, causal, heads):
    o, lse = _attn_fwd_call(q, k, v, scale, causal, heads)
    return o, (q, k, v, o, lse)


def _attention_bwd(scale, causal, heads, saved, do):
    q, k, v, o, lse = saved
    return _attn_bwd_call(q, k, v, o, lse, do, scale, causal, heads)


attention.defvjp(_attention_fwd, _attention_bwd)


NNB = (((2,), (1,)), ((0,), (0,)))
NTB = (((2,), (2,)), ((0,), (0,)))
TNB = (((1,), (1,)), ((0,), (0,)))


def _dot3(a, b, dims):
    a_hi, b_hi = a.astype(BF16), b.astype(BF16)
    a_lo = (a - a_hi.astype(F32)).astype(BF16)
    b_lo = (b - b_hi.astype(F32)).astype(BF16)

    def dot(x, y):
        return lax.dot_general(x, y, dims, preferred_element_type=F32)

    return dot(a_hi, b_hi) + (dot(a_hi, b_lo) + dot(a_lo, b_hi))


@functools.partial(jax.custom_vjp, nondiff_argnums=(2,))
def _hdot(a, b, dims):
    return _dot3(a, b, dims)


def _hdot_fwd(a, b, dims):
    return _dot3(a, b, dims), (a, b)


def _hdot_bwd(dims, saved, g):
    a, b = saved
    if dims == NNB:
        return _dot3(g, b, NTB), _dot3(a, g, TNB)
    if dims == NTB:
        return _dot3(g, b, NNB), _dot3(g, a, TNB)
    assert dims == TNB
    return _dot3(b, g, NTB), _dot3(a, g, NNB)


_hdot.defvjp(_hdot_fwd, _hdot_bwd)


def _dn_chunk(q, k, v, gcol, grow, bcol, state):
    c = q.shape[1]
    r_i = lax.broadcasted_iota(jnp.int32, (1, c, c), 1)
    c_i = lax.broadcasted_iota(jnp.int32, (1, c, c), 2)
    incl = c_i <= r_i
    strict = c_i < r_i
    g_cum_col = jnp.sum(jnp.where(incl, grow, 0.0), axis=2, keepdims=True)
    g_cum_row = jnp.sum(jnp.where(r_i <= c_i, gcol, 0.0), axis=1, keepdims=True)
    g_last = jnp.sum(grow, axis=2, keepdims=True)
    decay = jnp.where(incl, jnp.exp(jnp.where(incl, g_cum_col - g_cum_row, 0.0)), 0.0)
    qs = q * (q.shape[2] ** -0.5)
    kb = k * bcol
    x = -jnp.where(strict, _hdot(kb, k, NTB) * decay, 0.0)
    eye = (r_i == c_i).astype(F32)
    t = eye + x
    steps = max(1, int(np.ceil(np.log2(c))) - 1)
    for _ in range(steps):
        x = _hdot(x, x, NNB)
        t = t + _hdot(t, x, NNB)
    e_col = jnp.exp(g_cum_col)
    u = _hdot(t, v * bcol, NNB)
    w = _hdot(t, kb * e_col, NNB)
    attn = _hdot(qs, k, NTB) * decay
    v_new = u - _hdot(w, state, NNB)
    o = _hdot(qs * e_col, state, NNB) + _hdot(attn, v_new, NNB)
    k_dec = k * jnp.exp(g_last - g_cum_col)
    new_state = state * jnp.exp(g_last) + _hdot(k_dec, v_new, TNB)
    return o, new_state


DN_HEADS_PER_STEP = 8


def _dn_specs(hb, D, chunk_index):
    seq = pl.BlockSpec((CHUNK, hb * D), lambda h, n: (chunk_index(n), h))
    col = pl.BlockSpec((hb, CHUNK, 1), lambda h, n: (h, chunk_index(n), 0))
    row = pl.BlockSpec((hb, None, 1, CHUNK), lambda h, n: (h, chunk_index(n), 0, 0))
    st = pl.BlockSpec((hb, None, D, D), lambda h, n: (h, chunk_index(n), 0, 0))
    return seq, col, row, st


def _dn_heads(ref, hb, D):
    return jnp.stack([ref[:, h * D:(h + 1) * D] for h in range(hb)])


def _dn_fwd_call(q, k, v, gcol, grow, bcol):
    H = gcol.shape[0]
    S, D = q.shape[0], q.shape[1] // H
    N = S // CHUNK
    hb = _pick(H, (DN_HEADS_PER_STEP, 2, 1))
    seq, col, row, st = _dn_specs(hb, D, lambda n: n)

    def body(q_ref, k_ref, v_ref, gc_ref, gr_ref, bc_ref, o_ref, st_ref, state):
        @pl.when(pl.program_id(1) == 0)
        def _():
            state[...] = jnp.zeros_like(state)

        s_in = state[...]
        st_ref[...] = s_in
        o, s_out = _dn_chunk(_dn_heads(q_ref, hb, D), _dn_heads(k_ref, hb, D), _dn_heads(v_ref, hb, D),
                             gc_ref[...], gr_ref[...], bc_ref[...], s_in)
        for h in range(hb):
            o_ref[:, h * D:(h + 1) * D] = o[h]
        state[...] = s_out

    return pl.pallas_call(
        body,
        out_shape=(jax.ShapeDtypeStruct((S, H * D), F32), jax.ShapeDtypeStruct((H, N, D, D), F32)),
        grid=(H // hb, N), in_specs=[seq, seq, seq, col, row, col], out_specs=(seq, st),
        scratch_shapes=[pltpu.VMEM((hb, D, D), F32)], name="deltanet_fwd",
        compiler_params=_params(("parallel", "arbitrary")))(q, k, v, gcol, grow, bcol)


def _dn_bwd_call(q, k, v, gcol, grow, bcol, states, do):
    H = gcol.shape[0]
    S, D = q.shape[0], q.shape[1] // H
    N = S // CHUNK
    hb = _pick(H, (DN_HEADS_PER_STEP, 2, 1))
    seq, col, row, st = _dn_specs(hb, D, lambda n: N - 1 - n)

    def body(q_ref, k_ref, v_ref, gc_ref, gr_ref, bc_ref, st_ref, do_ref,
             dq_ref, dk_ref, dv_ref, dgc_ref, dgr_ref, dbc_ref, dstate):
        @pl.when(pl.program_id(1) == 0)
        def _():
            dstate[...] = jnp.zeros_like(dstate)

        _, vjp = jax.vjp(_dn_chunk, _dn_heads(q_ref, hb, D), _dn_heads(k_ref, hb, D), _dn_heads(v_ref, hb, D),
                         gc_ref[...], gr_ref[...], bc_ref[...], st_ref[...])
        grads = vjp((_dn_heads(do_ref, hb, D), dstate[...]))
        for ref, val in zip((dq_ref, dk_ref, dv_ref), grads[:3]):
            for h in range(hb):
                ref[:, h * D:(h + 1) * D] = val[h]
        for ref, val in zip((dgc_ref, dgr_ref, dbc_ref, dstate), grads[3:]):
            ref[...] = val

    sd = jax.ShapeDtypeStruct
    return pl.pallas_call(
        body,
        out_shape=(sd((S, H * D), F32), sd((S, H * D), F32), sd((S, H * D), F32),
                   sd((H, S, 1), F32), sd((H, N, 1, CHUNK), F32), sd((H, S, 1), F32)),
        grid=(H // hb, N), in_specs=[seq, seq, seq, col, row, col, st, seq],
        out_specs=(seq, seq, seq, col, row, col),
        scratch_shapes=[pltpu.VMEM((hb, D, D), F32)], name="deltanet_bwd",
        compiler_params=_params(("parallel", "arbitrary")))(q, k, v, gcol, grow, bcol, states, do)


@jax.custom_vjp
def delta_rule(q, k, v, gcol, grow, bcol):
    return _dn_fwd_call(q, k, v, gcol, grow, bcol)[0]


def _delta_rule_fwd(q, k, v, gcol, grow, bcol):
    o, states = _dn_fwd_call(q, k, v, gcol, grow, bcol)
    return o, (q, k, v, gcol, grow, bcol, states)


def _delta_rule_bwd(saved, do):
    return _dn_bwd_call(*saved, do)


delta_rule.defvjp(_delta_rule_fwd, _delta_rule_bwd)


def _shift_down(x, j):
    if j == 0:
        return x
    rows = lax.broadcasted_iota(jnp.int32, x.shape, 0)
    return jnp.where(rows >= j, pltpu.roll(x, j, 0), 0.0)


def _shift_up(x, j):
    if j == 0:
        return x
    s = x.shape[0]
    rows = lax.broadcasted_iota(jnp.int32, x.shape, 0)
    return jnp.where(rows < s - j, pltpu.roll(x, s - j, 0), 0.0)


def _conv(x, w):
    kw = w.shape[0]
    acc = x * w[kw - 1:kw, :]
    for kk in range(kw - 1):
        acc = acc + _shift_down(x, kw - 1 - kk) * w[kk:kk + 1, :]
    return acc


def _conv_t(d, w):
    kw = w.shape[0]
    acc = d * w[kw - 1:kw, :]
    for kk in range(kw - 1):
        acc = acc + _shift_up(d, kw - 1 - kk) * w[kk:kk + 1, :]
    return acc


def _conv_dw_rows(d, x, kw):
    return [jnp.sum(d * _shift_down(x, kw - 1 - kk), axis=0, keepdims=True) for kk in range(kw)]


def _silu(a):
    return a * jax.nn.sigmoid(a)


def _dsilu(a):
    s = jax.nn.sigmoid(a)
    return s * (1.0 + a * (1.0 - s))


def _conv_silu_fwd_call(x, w):
    S, C = x.shape
    kw = w.shape[0]
    tc = _pick(C, (256, 128))

    def body(x_ref, w_ref, y_ref):
        y_ref[...] = _silu(_conv(x_ref[...], w_ref[...]))

    xs = pl.BlockSpec((S, tc), lambda j: (0, j))
    ws = pl.BlockSpec((kw, tc), lambda j: (0, j))
    return pl.pallas_call(body, out_shape=jax.ShapeDtypeStruct((S, C), F32), grid=(C // tc,),
                          in_specs=[xs, ws], out_specs=xs, name="conv_silu_fwd",
                          compiler_params=_params(("parallel",)))(x, w)


def _conv_silu_bwd_call(x, w, dy):
    S, C = x.shape
    kw = w.shape[0]
    tc = _pick(C, (256, 128))

    def body(x_ref, w_ref, dy_ref, dx_ref, dw_ref):
        xv = x_ref[...]
        wv = w_ref[...]
        da = dy_ref[...] * _dsilu(_conv(xv, wv))
        dx_ref[...] = _conv_t(da, wv)
        for kk, row in enumerate(_conv_dw_rows(da, xv, kw)):
            dw_ref[kk:kk + 1, :] = row

    xs = pl.BlockSpec((S, tc), lambda j: (0, j))
    ws = pl.BlockSpec((kw, tc), lambda j: (0, j))
    return pl.pallas_call(
        body, out_shape=(jax.ShapeDtypeStruct((S, C), F32), jax.ShapeDtypeStruct((kw, C), F32)),
        grid=(C // tc,), in_specs=[xs, ws, xs], out_specs=(xs, ws), name="conv_silu_bwd",
        compiler_params=_params(("parallel",)))(x, w, dy)


@jax.custom_vjp
def conv_silu(x, w):
    return _conv_silu_fwd_call(x, w)


def _conv_silu_fwd(x, w):
    return conv_silu(x, w), (x, w)


def _conv_silu_bwd(saved, dy):
    return _conv_silu_bwd_call(*saved, dy)


conv_silu.defvjp(_conv_silu_fwd, _conv_silu_bwd)


def _conv_glu_fwd_call(x, w, b, tc):
    S, C2 = x.shape
    kw = w.shape[0]
    nb = C2 // (2 * tc)

    def body(x_ref, w_ref, b_ref, y_ref):
        a = _conv(x_ref[...], w_ref[...]) + b_ref[...]
        y_ref[...] = _silu(a[:, :tc]) * a[:, tc:]

    return pl.pallas_call(
        body, out_shape=jax.ShapeDtypeStruct((S, C2 // 2), F32), grid=(nb,),
        in_specs=[pl.BlockSpec((S, 2 * tc), lambda j: (0, j)), pl.BlockSpec((kw, 2 * tc), lambda j: (0, j)),
                  pl.BlockSpec((1, 2 * tc), lambda j: (0, j))],
        out_specs=pl.BlockSpec((S, tc), lambda j: (0, j)), name="conv_glu_fwd",
        compiler_params=_params(("parallel",)))(x, w, b)


def _conv_glu_bwd_call(x, w, b, dy, tc):
    S, C2 = x.shape
    kw = w.shape[0]
    nb = C2 // (2 * tc)

    def body(x_ref, w_ref, b_ref, dy_ref, dx_ref, dw_ref, db_ref):
        xv = x_ref[...]
        wv = w_ref[...]
        a = _conv(xv, wv) + b_ref[...]
        ag, au = a[:, :tc], a[:, tc:]
        dyv = dy_ref[...]
        da = jnp.concatenate([dyv * au * _dsilu(ag), dyv * _silu(ag)], axis=1)
        dx_ref[...] = _conv_t(da, wv)
        for kk, row in enumerate(_conv_dw_rows(da, xv, kw)):
            dw_ref[kk:kk + 1, :] = row
        db_ref[...] = jnp.sum(da, axis=0, keepdims=True)

    xs = pl.BlockSpec((S, 2 * tc), lambda j: (0, j))
    ws = pl.BlockSpec((kw, 2 * tc), lambda j: (0, j))
    bs = pl.BlockSpec((1, 2 * tc), lambda j: (0, j))
    return pl.pallas_call(
        body, out_shape=(jax.ShapeDtypeStruct((S, C2), F32), jax.ShapeDtypeStruct((kw, C2), F32),
                         jax.ShapeDtypeStruct((1, C2), F32)),
        grid=(nb,), in_specs=[xs, ws, bs, pl.BlockSpec((S, tc), lambda j: (0, j))], out_specs=(xs, ws, bs),
        name="conv_glu_bwd", compiler_params=_params(("parallel",)))(x, w, b, dy)


@functools.partial(jax.custom_vjp, nondiff_argnums=(3,))
def conv_glu(x, w, b, tc):
    return _conv_glu_fwd_call(x, w, b, tc)


def _conv_glu_fwd(x, w, b, tc):
    return conv_glu(x, w, b, tc), (x, w, b)


def _conv_glu_bwd(tc, saved, dy):
    return _conv_glu_bwd_call(*saved, dy, tc)


conv_glu.defvjp(_conv_glu_fwd, _conv_glu_bwd)


def _sqerr_call(y, t):
    S, D = y.shape
    tr = _row_tile(S, D)

    def body(y_ref, t_ref, l_ref, dy_ref):
        d = y_ref[...] - t_ref[...]
        dy_ref[...] = d * (1.0 / D)

        @pl.when(pl.program_id(0) == 0)
        def _():
            l_ref[...] = jnp.zeros_like(l_ref)

        l_ref[...] += jnp.full((8, LANES), 0.5 / D, F32) * jnp.sum(d * d)

    row = pl.BlockSpec((tr, D), lambda i: (i, 0))
    one = pl.BlockSpec((8, LANES), lambda i: (0, 0))
    return pl.pallas_call(
        body, out_shape=(jax.ShapeDtypeStruct((8, LANES), F32), jax.ShapeDtypeStruct((S, D), F32)),
        grid=(S // tr,), in_specs=[row, row], out_specs=(one, row), name="loss_head",
        compiler_params=_params(("arbitrary",)))(y, t)


@jax.custom_vjp
def sq_loss(y, t):
    return _sqerr_call(y, t)[0][0, 0]


def _sq_loss_fwd(y, t):
    l, dy = _sqerr_call(y, t)
    return l[0, 0], dy


def _sq_loss_bwd(dy, dl):
    return dy * dl, None


sq_loss.defvjp(_sq_loss_fwd, _sq_loss_bwd)


def adamw(w, g, m, v):
    shape = w.shape
    C = shape[-1]
    R = int(np.prod(shape[:-1]))
    tr = _pick(R, tuple(t for t in (4096, 2048, 1024, 512, 256, 128, 64, 32, 16, 8) if t * C * 4 <= (1 << 20)))
    c1 = 1.0 / (1.0 - ADAM_B1 ** ADAM_STEP)
    c2 = 1.0 / (1.0 - ADAM_B2 ** ADAM_STEP)

    def body(w_ref, g_ref, m_ref, v_ref, d_ref, nm_ref, nv_ref):
        gv = g_ref[...]
        nm = ADAM_B1 * m_ref[...] + (1.0 - ADAM_B1) * gv
        nv = ADAM_B2 * v_ref[...] + (1.0 - ADAM_B2) * (gv * gv)
        d_ref[...] = -ADAM_LR * ((nm * c1) / (jnp.sqrt(nv * c2) + ADAM_EPS) + ADAM_WD * w_ref[...])
        nm_ref[...] = nm
        nv_ref[...] = nv

    blk = pl.BlockSpec((tr, C), lambda i: (i, 0))
    out = jax.ShapeDtypeStruct((R, C), F32)
    d, nm, nv = pl.pallas_call(body, out_shape=(out, out, out), grid=(R // tr,), in_specs=[blk] * 4,
                               out_specs=(blk, blk, blk), name="adamw",
                               compiler_params=_params(("parallel",)))(
        w.reshape(R, C), g.reshape(R, C), m.reshape(R, C), v.reshape(R, C))
    return d.reshape(shape), nm.reshape(shape), nv.reshape(shape)


def _place():
    x, y, c = lax.axis_index("x"), lax.axis_index("y"), lax.axis_index("c")
    chips = [(1 - x, y), (x, 1 - y), (1 - x, 1 - y)]
    return x, y, c, chips


def small_allgather(buf):
    m_per, n = buf.shape

    def body(x_ref, out_ref, send_sems, recv_sems, local_sem):
        x, y, c, chips = _place()
        me, sibling = (x, y, c), (x, y, 1 - c)

        def rows(px, py, pc):
            return out_ref.at[4 * px + 2 * py + pc]

        def copy(k, block, to, src=None):
            return pltpu.make_async_remote_copy(
                src_ref=rows(*block) if src is None else src, dst_ref=rows(*block),
                send_sem=send_sems.at[k], recv_sem=recv_sems.at[k], device_id=to, device_id_type=MESH)

        mine = pltpu.make_async_copy(x_ref, rows(*me), local_sem)
        mine.start()
        first = [copy(0, me, sibling, src=x_ref)]
        first += [copy(1 + j, me, (*chip, c), src=x_ref) for j, chip in enumerate(chips)]
        for cp in first:
            cp.start()
        passed = [copy(4 + j, (*chip, c), sibling) for j, chip in enumerate(chips)]
        for j, chip in enumerate(chips):
            copy(1 + j, (*chip, c), me).wait_recv()
            passed[j].start()
        copy(0, sibling, me).wait_recv()
        for j, chip in enumerate(chips):
            copy(4 + j, (*chip, 1 - c), me).wait_recv()
        for cp in first + passed:
            cp.wait_send()
        mine.wait()

    return pl.pallas_call(
        body, out_shape=jax.ShapeDtypeStruct((N_DEV, m_per, n), buf.dtype),
        in_specs=[pl.BlockSpec(memory_space=pltpu.VMEM)], out_specs=pl.BlockSpec(memory_space=pltpu.VMEM),
        scratch_shapes=[pltpu.SemaphoreType.DMA((7,)), pltpu.SemaphoreType.DMA((7,)), pltpu.SemaphoreType.DMA],
        name="small_allgather", compiler_params=pltpu.CompilerParams(vmem_limit_bytes=VMEM_LIMIT))(buf)


def sum_blocks(g):
    n, R, C = g.shape
    tr = _pick(R, (512, 256, 128, 64, 32, 16, 8))

    def body(g_ref, o_ref):
        acc = g_ref[0].astype(F32)
        for d in range(1, n):
            acc = acc + g_ref[d].astype(F32)
        o_ref[...] = acc

    return pl.pallas_call(body, out_shape=jax.ShapeDtypeStruct((R, C), F32), grid=(R // tr,),
                          in_specs=[pl.BlockSpec((n, tr, C), lambda i: (0, i, 0))],
                          out_specs=pl.BlockSpec((tr, C), lambda i: (i, 0)), name="sum_blocks",
                          compiler_params=_params(("parallel",)))(g)


def _hbm_specs(n):
    return [pl.BlockSpec(memory_space=pl.ANY)] * n


def gather_weights(shards):
    n = len(shards)

    def body(*refs):
        src, out = refs[:n], refs[n:2 * n]
        send_sems, recv_sems = refs[2 * n:]
        x, y, c, chips = _place()
        xn, yn, dg = chips
        sibling = (x, y, 1 - c)
        my_chip = 2 * x + y
        slot = lambda chip: 2 * chip[0] + chip[1]
        started = []

        def copy(t, k, ref, to):
            return pltpu.make_async_remote_copy(
                src_ref=ref, dst_ref=ref, send_sem=send_sems.at[t, k], recv_sem=recv_sems.at[t, k],
                device_id=to, device_id_type=MESH)

        def start(cp):
            cp.start()
            started.append(cp)

        def rows(t, chip, first, count):
            return out[t].at[slot(chip), pl.ds(first, count)]

        for t in range(n):
            half = src[t].shape[0] // 2
            for k, chip in enumerate((xn, yn)):
                cp = pltpu.make_async_remote_copy(
                    src_ref=src[t].at[pl.ds(c * half, half)], dst_ref=out[t].at[my_chip, pl.ds(c * half, half)],
                    send_sem=send_sems.at[t, k], recv_sem=recv_sems.at[t, k], device_id=(*chip, c), device_id_type=MESH)
                start(cp)
        for t in range(n):
            half = src[t].shape[0] // 2
            quarter = half // 2
            copy(t, 0, rows(t, xn, c * half, half), (*xn, c)).wait_recv()
            start(copy(t, 2, rows(t, xn, c * half, quarter), (*yn, c)))
            start(copy(t, 4, rows(t, xn, c * half, half), sibling))
            copy(t, 1, rows(t, yn, c * half, half), (*yn, c)).wait_recv()
            start(copy(t, 3, rows(t, yn, c * half + quarter, quarter), (*xn, c)))
            start(copy(t, 5, rows(t, yn, c * half, half), sibling))
        for t in range(n):
            half = src[t].shape[0] // 2
            quarter = half // 2
            copy(t, 2, rows(t, dg, c * half, quarter), (*yn, c)).wait_recv()
            copy(t, 3, rows(t, dg, c * half + quarter, quarter), (*xn, c)).wait_recv()
            start(copy(t, 6, rows(t, dg, c * half, half), sibling))
        for t in range(n):
            half = src[t].shape[0] // 2
            for k, chip in ((4, xn), (5, yn), (6, dg)):
                copy(t, k, rows(t, chip, (1 - c) * half, half), sibling).wait_recv()
        for cp in started:
            cp.wait_send()

    out_shape = tuple(jax.ShapeDtypeStruct((N_CHIPS,) + s.shape, s.dtype) for s in shards)
    return pl.pallas_call(
        body, out_shape=out_shape, in_specs=_hbm_specs(n), out_specs=tuple(_hbm_specs(n)),
        scratch_shapes=[pltpu.SemaphoreType.DMA((n, 7)), pltpu.SemaphoreType.DMA((n, 7))],
        name="gather_weights", compiler_params=pltpu.CompilerParams(has_side_effects=True))(*shards)


def sibling_exchange_halves(grads):
    n = len(grads)

    def body(*refs):
        src, out = refs[:n], refs[n:2 * n]
        send_sems, recv_sems = refs[2 * n:]
        x, y, c, _ = _place()
        sibling = (x, y, 1 - c)
        cps = []
        for t in range(n):
            half = src[t].shape[1] // 2
            cp = pltpu.make_async_remote_copy(
                src_ref=src[t].at[:, pl.ds((1 - c) * half, half)], dst_ref=out[t],
                send_sem=send_sems.at[t], recv_sem=recv_sems.at[t], device_id=sibling, device_id_type=MESH)
            cp.start()
            cps.append(cp)
        for cp in cps:
            cp.wait()

    out_shape = tuple(jax.ShapeDtypeStruct((s.shape[0], s.shape[1] // 2, s.shape[2]), s.dtype) for s in grads)
    return pl.pallas_call(
        body, out_shape=out_shape, in_specs=_hbm_specs(n), out_specs=tuple(_hbm_specs(n)),
        scratch_shapes=[pltpu.SemaphoreType.DMA((n,)), pltpu.SemaphoreType.DMA((n,))],
        name="sibling_exchange_halves", compiler_params=pltpu.CompilerParams(has_side_effects=True))(*grads)


def pair_sum(g, recv, c_idx):
    G, R, C = g.shape
    half = R // 2
    tr = _pick(half, tuple(t for t in (1024, 512, 256, 128, 64, 32, 16) if t * C * 2 <= (4 << 20)))
    g4 = g.reshape(G, 2, half, C)

    def body(c_ref, g_ref, r_ref, o_ref):
        o_ref[...] = (g_ref[...].astype(F32) + r_ref[...].astype(F32)).astype(o_ref.dtype)

    grid_spec = pltpu.PrefetchScalarGridSpec(
        num_scalar_prefetch=1, grid=(G, half // tr),
        in_specs=[pl.BlockSpec((None, None, tr, C), lambda i, j, c_ref: (i, c_ref[0], j, 0)),
                  pl.BlockSpec((None, tr, C), lambda i, j, c_ref: (i, j, 0))],
        out_specs=pl.BlockSpec((None, tr, C), lambda i, j, c_ref: (i, j, 0)))
    return pl.pallas_call(body, out_shape=jax.ShapeDtypeStruct((G, half, C), g.dtype), grid_spec=grid_spec,
                          name="pair_sum", compiler_params=_params(("parallel", "parallel")))(c_idx, g4, recv)


def chip_exchange(parts):
    n = len(parts)

    def body(*refs):
        src, out = refs[:n], refs[n:2 * n]
        send_sems, recv_sems = refs[2 * n:]
        x, y, c, chips = _place()
        cps = []
        for t in range(n):
            for j, chip in enumerate(chips):
                cp = pltpu.make_async_remote_copy(
                    src_ref=src[t].at[2 * chip[0] + chip[1]], dst_ref=out[t].at[j],
                    send_sem=send_sems.at[t, j], recv_sem=recv_sems.at[t, j],
                    device_id=(*chip, c), device_id_type=MESH)
                cp.start()
                cps.append(cp)
        for cp in cps:
            cp.wait()

    out_shape = tuple(jax.ShapeDtypeStruct((3,) + s.shape[1:], s.dtype) for s in parts)
    return pl.pallas_call(
        body, out_shape=out_shape, in_specs=_hbm_specs(n), out_specs=tuple(_hbm_specs(n)),
        scratch_shapes=[pltpu.SemaphoreType.DMA((n, 3)), pltpu.SemaphoreType.DMA((n, 3))],
        name="chip_exchange", compiler_params=pltpu.CompilerParams(has_side_effects=True))(*parts)


def sum_slots_into(parts, recv, chip_idx, acc, layer, depth):
    _, H, C = parts.shape
    tr = _pick(H, tuple(t for t in (1024, 512, 256, 128, 64, 32, 16) if t * C * 4 <= (1 << 20)))
    has_acc = acc is not None

    def body(chip_ref, p_ref, r_ref, *rest):
        o_ref = rest[-1]
        o_ref[...] = ((r_ref[0].astype(F32) + r_ref[1].astype(F32)) + r_ref[2].astype(F32)) + p_ref[...].astype(F32)

    in_specs = [pl.BlockSpec((None, tr, C), lambda i, chip_ref: (chip_ref[0], i, 0)),
                pl.BlockSpec((3, tr, C), lambda i, chip_ref: (0, i, 0))]
    args = [chip_idx, parts, recv]
    if has_acc:
        in_specs.append(pl.BlockSpec(memory_space=pl.ANY))
        args.append(acc)
    grid_spec = pltpu.PrefetchScalarGridSpec(
        num_scalar_prefetch=1, grid=(H // tr,), in_specs=in_specs,
        out_specs=pl.BlockSpec((None, tr, C), lambda i, chip_ref: (layer, i, 0)))
    return pl.pallas_call(body, out_shape=jax.ShapeDtypeStruct((depth, H, C), F32), grid_spec=grid_spec,
                          input_output_aliases=({3: 0} if has_acc else {}), name="sum_slots",
                          compiler_params=_params(("parallel",)))(*args)


def sibling_swap(mine):
    n = len(mine)

    def body(*refs):
        src, out = refs[:n], refs[n:2 * n]
        send_sems, recv_sems = refs[2 * n:]
        x, y, c, _ = _place()
        cps = []
        for t in range(n):
            cp = pltpu.make_async_remote_copy(
                src_ref=src[t], dst_ref=out[t], send_sem=send_sems.at[t], recv_sem=recv_sems.at[t],
                device_id=(x, y, 1 - c), device_id_type=MESH)
            cp.start()
            cps.append(cp)
        for cp in cps:
            cp.wait()

    out_shape = tuple(jax.ShapeDtypeStruct(s.shape, s.dtype) for s in mine)
    return pl.pallas_call(
        body, out_shape=out_shape, in_specs=_hbm_specs(n), out_specs=tuple(_hbm_specs(n)),
        scratch_shapes=[pltpu.SemaphoreType.DMA((n,)), pltpu.SemaphoreType.DMA((n,))],
        name="sibling_swap", compiler_params=pltpu.CompilerParams(has_side_effects=True))(*mine)


def reduce_layer(grads, c_idx, chip_idx, accs, layer, depth):
    recv = sibling_exchange_halves(grads)
    parts = [pair_sum(g, r, c_idx) for g, r in zip(grads, recv)]
    slots = chip_exchange(parts)
    return [sum_slots_into(p, s, chip_idx, a, layer, depth) for p, s, a in zip(parts, slots, accs)]


def adamw_halves(w, mine, theirs, m, v, c_idx):
    L, R, C = w.shape
    half = R // 2
    tr = _pick(half, tuple(t for t in (2048, 1024, 512, 256, 128, 64, 32, 16, 8) if t * C * 4 <= (1 << 20)))
    nb = half // tr
    c1 = 1.0 / (1.0 - ADAM_B1 ** ADAM_STEP)
    c2 = 1.0 / (1.0 - ADAM_B2 ** ADAM_STEP)

    def body(c_ref, w_ref, a_ref, b_ref, m_ref, v_ref, g_ref, d_ref, nm_ref, nv_ref):
        gv = jnp.where(c_ref[0] == pl.program_id(1), a_ref[...], b_ref[...])
        nm = ADAM_B1 * m_ref[...] + (1.0 - ADAM_B1) * gv
        nv = ADAM_B2 * v_ref[...] + (1.0 - ADAM_B2) * (gv * gv)
        g_ref[...] = gv
        d_ref[...] = -ADAM_LR * ((nm * c1) / (jnp.sqrt(nv * c2) + ADAM_EPS) + ADAM_WD * w_ref[...])
        nm_ref[...] = nm
        nv_ref[...] = nv

    full = pl.BlockSpec((None, tr, C), lambda l, h, i, c_ref: (l, h * nb + i, 0))
    part = pl.BlockSpec((None, tr, C), lambda l, h, i, c_ref: (l, i, 0))
    grid_spec = pltpu.PrefetchScalarGridSpec(
        num_scalar_prefetch=1, grid=(L, 2, nb), in_specs=[full, part, part, full, full],
        out_specs=(full, full, full, full))
    out = jax.ShapeDtypeStruct((L, R, C), F32)
    return pl.pallas_call(body, out_shape=(out, out, out, out), grid_spec=grid_spec, name="adamw_halves",
                          compiler_params=_params(("parallel", "parallel", "parallel")))(c_idx, w, mine, theirs, m, v)


BIG = ("w_in", "mla_w_qb", "mla_w_kvb", "w_out", "xa_wq", "xa_wk", "xa_wv", "xa_wo", "ffn_w_up", "ffn_w_down")
COL_SHARDED = ("w_in", "mla_w_qb", "mla_w_kvb", "ffn_w_up")
SMALL_REPL = ("norm_mix", "dn_a_log", "dn_dt_bias", "dn_out_norm", "mla_q_norm", "mla_kv_norm", "mem_norm",
              "norm_xattn", "norm_ffn", "ffn_conv_bias", "norm_final")
SMALL_SHARDED = ("dn_conv", "ffn_conv")
WEIGHTS = ("norm_mix", "w_in", "dn_conv", "dn_a_log", "dn_dt_bias", "dn_out_norm", "mla_q_norm", "mla_w_qb",
           "mla_kv_norm", "mla_w_kvb", "w_out", "mem_norm", "norm_xattn", "xa_wq", "xa_wk", "xa_wv", "xa_wo",
           "norm_ffn", "ffn_w_up", "ffn_conv", "ffn_conv_bias", "ffn_w_down", "norm_final")


def _pad_cols(a, cp):
    c = a.shape[-1]
    if c == cp:
        return a
    return jnp.pad(a, [(0, 0)] * (a.ndim - 1) + [(0, cp - c)])


def _pack(arrs):
    flat = jnp.concatenate([a.reshape(-1).astype(F32) for a in arrs])
    rows = -(-flat.shape[0] // LANES)
    rows = -(-rows // 8) * 8
    return jnp.pad(flat, (0, rows * LANES - flat.shape[0])).reshape(rows, LANES)


def _unpack(buf, like):
    flat = buf.reshape(-1)
    out, off = [], 0
    for a in like:
        n = int(np.prod(a.shape))
        out.append(flat[off:off + n].reshape(a.shape))
        off += n
    return out


def _heads(t, h):
    s = t.shape[0]
    return jnp.transpose(t.reshape(s, h, t.shape[1] // h), (1, 0, 2))


def _unheads(t):
    h, s, d = t.shape
    return jnp.transpose(t, (1, 0, 2)).reshape(s, h * d)


def _rope(t, cos, sin):
    t1, t2 = jnp.split(t, 2, axis=-1)
    return jnp.concatenate([t1 * cos - t2 * sin, t2 * cos + t1 * sin], axis=-1)


def _forward_loss(p, x, mem, cos, sin, target, dims):
    S, D = x.shape
    depth, dn_h, mla_h, d_ff, c_in, c_in_pad, ffn_tn = dims
    dn_w = dn_h * DN_HEAD_DIM
    n_chunks = S // CHUNK
    mem_n = rms_norm(mem, p["mem_norm"], EPS)
    h = x
    ffn_blocks = (2 * d_ff) // ffn_tn
    half_blocks = ffn_blocks // 2

    def ffn_perm(nb):
        return jnp.where(nb < half_blocks, 2 * nb, 2 * (nb - half_blocks) + 1)

    def ffn_layout(a):
        k = a.shape[0]
        return jnp.transpose(a.reshape(k, 2, half_blocks, ffn_tn), (0, 2, 1, 3)).reshape(k, 2 * d_ff)

    for l in range(depth):
        u = rms_norm(h, p["norm_mix"][l], EPS)
        projp = linear(u, p["w_in"][l])
        proj = jnp.concatenate([projp[:, g * c_in_pad:g * c_in_pad + c_in] for g in range(N_CHIPS)], axis=1)
        o0 = 3 * dn_w
        dz = proj[:, o0:o0 + dn_w]
        db = proj[:, o0 + dn_w:o0 + dn_w + dn_h]
        da = proj[:, o0 + dn_w + dn_h:o0 + dn_w + 2 * dn_h]
        o1 = o0 + dn_w + 2 * dn_h
        mq = proj[:, o1:o1 + MLA_Q_RANK]
        mkv = proj[:, o1 + MLA_Q_RANK:]
        qkv = conv_silu(proj[:, :o0], p["dn_conv"][l])
        qd = group_l2_norm(qkv[:, :dn_w], dn_h)
        kd = group_l2_norm(qkv[:, dn_w:2 * dn_w], dn_h)
        vd = qkv[:, 2 * dn_w:]
        beta = jax.nn.sigmoid(db)
        g = -jnp.exp(p["dn_a_log"][l]) * jax.nn.softplus(da + p["dn_dt_bias"][l])
        g_t, beta_t = g.T, beta.T
        o_dn = delta_rule(qd, kd, vd, g_t[:, :, None], g_t.reshape(dn_h, n_chunks, 1, CHUNK), beta_t[:, :, None])
        o_dn = group_rms_norm(o_dn, p["dn_out_norm"][l], dn_h, EPS)
        o_dn = o_dn * jax.nn.silu(dz)
        qf = linear(rms_norm(mq, p["mla_q_norm"][l], EPS), p["mla_w_qb"][l]).reshape(S, mla_h, MLA_NOPE + MLA_ROPE)
        q_pe = _rope(qf[..., MLA_NOPE:], cos[:, None, :], sin[:, None, :])
        k_pe = _rope(mkv[:, MLA_KV_RANK:], cos, sin)
        kv = linear(rms_norm(mkv[:, :MLA_KV_RANK], p["mla_kv_norm"][l], EPS), p["mla_w_kvb"][l])
        kv = kv.reshape(S, mla_h, MLA_NOPE + MLA_V)
        qa = jnp.transpose(jnp.concatenate([qf[..., :MLA_NOPE], q_pe], axis=-1), (1, 0, 2))
        ka = jnp.transpose(jnp.concatenate(
            [kv[..., :MLA_NOPE], jnp.broadcast_to(k_pe[:, None, :], (S, mla_h, MLA_ROPE))], axis=-1), (1, 0, 2))
        va = jnp.transpose(kv[..., MLA_NOPE:], (1, 0, 2))
        o_mla = _unheads(attention(qa, ka, va, (MLA_NOPE + MLA_ROPE) ** -0.5, True, None))
        h = linear(jnp.concatenate([o_dn, o_mla], axis=-1), p["w_out"][l], res=h)
        hn = rms_norm(h, p["norm_xattn"][l], EPS)
        xo = attention(linear(hn, p["xa_wq"][l]), linear(mem_n, p["xa_wk"][l]), linear(mem_n, p["xa_wv"][l]),
                       (D // XA_HEADS) ** -0.5, False, XA_HEADS)
        h = linear(xo, p["xa_wo"][l], res=h)
        hn = rms_norm(h, p["norm_ffn"][l], EPS)
        pre = linear(hn, p["ffn_w_up"][l], tn=ffn_tn, colperm=ffn_perm)
        act = conv_glu(pre, ffn_layout(p["ffn_conv"][l]), ffn_layout(p["ffn_conv_bias"][l][None, :]), ffn_tn)
        h = linear(act, p["ffn_w_down"][l], res=h)
    y = rms_norm(h, p["norm_final"], EPS)
    return sq_loss(y, target)


def kernel(x, mem, positions, norm_mix, w_in, dn_conv, dn_a_log, dn_dt_bias, dn_out_norm, mla_q_norm, mla_w_qb, mla_kv_norm, mla_w_kvb, w_out, mem_norm, norm_xattn, xa_wq, xa_wk, xa_wv, xa_wo, norm_ffn, ffn_w_up, ffn_conv, ffn_conv_bias, ffn_w_down, norm_final, loss_target, m_norm_mix, m_w_in, m_dn_conv, m_dn_a_log, m_dn_dt_bias, m_dn_out_norm, m_mla_q_norm, m_mla_w_qb, m_mla_kv_norm, m_mla_w_kvb, m_w_out, m_mem_norm, m_norm_xattn, m_xa_wq, m_xa_wk, m_xa_wv, m_xa_wo, m_norm_ffn, m_ffn_w_up, m_ffn_conv, m_ffn_conv_bias, m_ffn_w_down, m_norm_final, v_norm_mix, v_w_in, v_dn_conv, v_dn_a_log, v_dn_dt_bias, v_dn_out_norm, v_mla_q_norm, v_mla_w_qb, v_mla_kv_norm, v_mla_w_kvb, v_w_out, v_mem_norm, v_norm_xattn, v_xa_wq, v_xa_wk, v_xa_wv, v_xa_wo, v_norm_ffn, v_ffn_w_up, v_ffn_conv, v_ffn_conv_bias, v_ffn_w_down, v_norm_final):
    w = dict(norm_mix=norm_mix, w_in=w_in, dn_conv=dn_conv, dn_a_log=dn_a_log, dn_dt_bias=dn_dt_bias, dn_out_norm=dn_out_norm, mla_q_norm=mla_q_norm, mla_w_qb=mla_w_qb, mla_kv_norm=mla_kv_norm, mla_w_kvb=mla_w_kvb, w_out=w_out, mem_norm=mem_norm, norm_xattn=norm_xattn, xa_wq=xa_wq, xa_wk=xa_wk, xa_wv=xa_wv, xa_wo=xa_wo, norm_ffn=norm_ffn, ffn_w_up=ffn_w_up, ffn_conv=ffn_conv, ffn_conv_bias=ffn_conv_bias, ffn_w_down=ffn_w_down, norm_final=norm_final)
    m = dict(norm_mix=m_norm_mix, w_in=m_w_in, dn_conv=m_dn_conv, dn_a_log=m_dn_a_log, dn_dt_bias=m_dn_dt_bias, dn_out_norm=m_dn_out_norm, mla_q_norm=m_mla_q_norm, mla_w_qb=m_mla_w_qb, mla_kv_norm=m_mla_kv_norm, mla_w_kvb=m_mla_w_kvb, w_out=m_w_out, mem_norm=m_mem_norm, norm_xattn=m_norm_xattn, xa_wq=m_xa_wq, xa_wk=m_xa_wk, xa_wv=m_xa_wv, xa_wo=m_xa_wo, norm_ffn=m_norm_ffn, ffn_w_up=m_ffn_w_up, ffn_conv=m_ffn_conv, ffn_conv_bias=m_ffn_conv_bias, ffn_w_down=m_ffn_w_down, norm_final=m_norm_final)
    v = dict(norm_mix=v_norm_mix, w_in=v_w_in, dn_conv=v_dn_conv, dn_a_log=v_dn_a_log, dn_dt_bias=v_dn_dt_bias, dn_out_norm=v_dn_out_norm, mla_q_norm=v_mla_q_norm, mla_w_qb=v_mla_w_qb, mla_kv_norm=v_mla_kv_norm, mla_w_kvb=v_mla_w_kvb, w_out=v_w_out, mem_norm=v_mem_norm, norm_xattn=v_norm_xattn, xa_wq=v_xa_wq, xa_wk=v_xa_wk, xa_wv=v_xa_wv, xa_wo=v_xa_wo, norm_ffn=v_norm_ffn, ffn_w_up=v_ffn_w_up, ffn_conv=v_ffn_conv, ffn_conv_bias=v_ffn_conv_bias, ffn_w_down=v_ffn_w_down, norm_final=v_norm_final)

    S, D = x.shape[1], x.shape[2]
    depth = w_in.shape[0]
    dn_h = (D // 2) // DN_HEAD_DIM
    mla_h = (D - dn_h * DN_HEAD_DIM) // MLA_V
    d_ff = ffn_w_down.shape[1] * N_CHIPS
    c_in = w_in.shape[2]
    c_in_pad = -(-c_in // LANES) * LANES
    ffn_tn = _pick(ffn_w_up.shape[2], (256, 128))
    dims = (depth, dn_h, mla_h, d_ff, c_in, c_in_pad, ffn_tn)
    chip = 2 * lax.axis_index("x") + lax.axis_index("y")
    c_idx = lax.axis_index("c").astype(jnp.int32).reshape(1)

    gathered = {n: [] for n in BIG}
    for l in range(depth):
        shards = []
        for n in BIG:
            s = w[n][l].astype(BF16)
            if n == "w_in":
                s = _pad_cols(s, c_in_pad)
            shards.append(s)
        for n, own, full in zip(BIG, shards, gather_weights(shards)):
            full = lax.dynamic_update_slice(full, own[None], (chip, 0, 0))
            if n not in COL_SHARDED:
                full = full.reshape(1, N_CHIPS * full.shape[1], full.shape[2])
            gathered[n].append(full)
    small_sh = small_allgather(_pack([w[n] for n in SMALL_SHARDED]))
    params = dict(gathered)
    for n, parts in zip(SMALL_SHARDED, zip(*[_unpack(small_sh[2 * j], [w[k] for k in SMALL_SHARDED])
                                              for j in range(N_CHIPS)])):
        params[n] = jnp.concatenate(parts, axis=-1)
    for n in SMALL_REPL:
        params[n] = w[n]

    inv = ROPE_BASE ** (-jnp.arange(0, MLA_ROPE, 2, dtype=F32) / MLA_ROPE)
    ang = positions[0].astype(F32)[:, None] * inv
    cos, sin = jnp.cos(ang), jnp.sin(ang)
    loss_fn = lambda p, xx: _forward_loss(p, xx, mem[0], cos, sin, loss_target[0], dims)
    loss_local, (gp, gx) = jax.value_and_grad(loss_fn, argnums=(0, 1))(params, x[0])

    small_names = SMALL_REPL + SMALL_SHARDED
    packed = _pack([loss_local.reshape(1)] + [gp[n] for n in small_names])
    summed = sum_blocks(small_allgather(packed))
    unpacked = _unpack(summed, [loss_local.reshape(1)] + [gp[n] for n in small_names])
    loss = unpacked[0][0]
    grads = dict(zip(small_names, unpacked[1:]))
    for n in SMALL_SHARDED:
        cs = w[n].shape[-1]
        grads[n] = lax.dynamic_slice_in_dim(grads[n], chip * cs, cs, axis=-1)

    chip_idx = chip.astype(jnp.int32).reshape(1)
    accs = [None] * len(BIG)
    for l in range(depth):
        gl = []
        for n in BIG:
            g = gp[n][l]
            if n not in COL_SHARDED:
                g = g.reshape(N_CHIPS, g.shape[1] // N_CHIPS, g.shape[2])
            gl.append(g)
        accs = reduce_layer(gl, c_idx, chip_idx, accs, l, depth)
    theirs = sibling_swap(accs)

    delta, new_m, new_v = {}, {}, {}
    for n, mine, other in zip(BIG, accs, theirs):
        if n == "w_in":
            lo, hi = jnp.where(c_idx[0] == 0, mine, other), jnp.where(c_idx[0] == 0, other, mine)
            grads[n] = jnp.concatenate([lo, hi], axis=1)[:, :, :c_in]
            delta[n], new_m[n], new_v[n] = adamw(w[n], grads[n], m[n], v[n])
        else:
            grads[n], delta[n], new_m[n], new_v[n] = adamw_halves(w[n], mine, other, m[n], v[n], c_idx)
    sw, sg, sm, sv = (_pack([d[n] for n in small_names]) for d in (w, grads, m, v))
    sd, snm, snv = adamw(sw, sg, sm, sv)
    like = [w[n] for n in small_names]
    for d, buf in ((delta, sd), (new_m, snm), (new_v, snv)):
        for n, a in zip(small_names, _unpack(buf, like)):
            d[n] = a

    return (loss, gx[None], *[grads[n] for n in WEIGHTS], *[delta[n] for n in WEIGHTS],
            *[new_m[n] for n in WEIGHTS], *[new_v[n] for n in WEIGHTS])
```

```python
import functools

import numpy as np
import jax
import jax.numpy as jnp
from jax import lax
from jax.experimental import pallas as pl
from jax.experimental.pallas import tpu as pltpu

F32 = jnp.float32
BF16 = jnp.bfloat16
MESH = pl.DeviceIdType.MESH
HIGHEST = lax.Precision.HIGHEST

LANES = 128
VMEM_LIMIT = 56 * 1024 * 1024
N_CHIPS = 4
N_DEV = 8

CHUNK = 64
DN_HEAD_DIM = 128
MLA_NOPE, MLA_ROPE, MLA_V = 128, 64, 128
MLA_Q_RANK, MLA_KV_RANK = 512, 256
XA_HEADS = 4
ROPE_BASE = 10000.0
EPS = 1e-6

ADAM_LR, ADAM_B1, ADAM_B2, ADAM_EPS, ADAM_WD, ADAM_STEP = 0.001, 0.9, 0.999, 1e-08, 0.01, 10

NN = (((1,), (0,)), ((), ()))
NT = (((1,), (1,)), ((), ()))
TN = (((0,), (0,)), ((), ()))


def _pick(dim, cands):
    for c in cands:
        if c <= dim and dim % c == 0:
            return c
    return dim


def _params(sem=None):
    return pltpu.CompilerParams(dimension_semantics=sem, vmem_limit_bytes=VMEM_LIMIT)


def _mm_call(name, a, b, res, dims, grid, a_spec, b_spec, o_spec, out_shape, acc_shape):
    nk = grid[3]
    has_res = res is not None

    def body(*refs):
        if has_res:
            a_ref, b_ref, r_ref, o_ref, acc = refs
        else:
            a_ref, b_ref, o_ref, acc = refs
        kk = pl.program_id(3)

        @pl.when(kk == 0)
        def _():
            acc[...] = jnp.zeros_like(acc)

        acc[...] += lax.dot_general(a_ref[...].astype(BF16), b_ref[...].astype(BF16), dims,
                                    preferred_element_type=F32)

        @pl.when(kk == nk - 1)
        def _():
            r = acc[...]
            if has_res:
                r = r + r_ref[...]
            o_ref[...] = r.astype(o_ref.dtype)

    in_specs = [a_spec, b_spec] + ([o_spec] if has_res else [])
    args = (a, b) + ((res,) if has_res else ())
    return pl.pallas_call(
        body, out_shape=out_shape, grid=grid, in_specs=in_specs, out_specs=o_spec,
        scratch_shapes=[pltpu.VMEM(acc_shape, F32)], name=name,
        compiler_params=_params(("parallel", "parallel", "parallel", "arbitrary")),
    )(*args)


_TM = (1024, 512, 256, 128, 64, 32, 16, 8)
_TN = (1024, 768, 640, 512, 384, 256, 128)


def _ident(nb):
    return nb


def mm_nn_raw(a, w, res=None, tn=None, colperm=_ident, name="mm_nn"):
    M, K = a.shape
    G, _, C = w.shape
    tm = _pick(M, _TM)
    tn = tn or _pick(C, _TN)
    tk = K if K <= 2048 else _pick(K, (1408, 1024, 512, 256, 128))
    nj = C // tn
    grid = (G, M // tm, nj, K // tk)
    return _mm_call(
        name, a, w, res, NN, grid,
        pl.BlockSpec((tm, tk), lambda g, i, j, k: (i, k)),
        pl.BlockSpec((None, tk, tn), lambda g, i, j, k: (g, k, j)),
        pl.BlockSpec((tm, tn), lambda g, i, j, k: (i, colperm(g * nj + j))),
        jax.ShapeDtypeStruct((M, G * C), F32), (tm, tn))


def mm_nt_raw(d, w, tn=None, colperm=_ident, name="mm_nt"):
    M = d.shape[0]
    G, K, C = w.shape
    tm = _pick(M, _TM)
    tko = _pick(K, _TN)
    tc = tn or _pick(C, (2048, 1408, 1280, 1024, 512, 384, 256, 128))
    ncb = C // tc
    grid = (1, M // tm, K // tko, G * ncb)
    return _mm_call(
        name, d, w, None, NT, grid,
        pl.BlockSpec((tm, tc), lambda g, i, j, k: (i, colperm(k))),
        pl.BlockSpec((None, tko, tc), lambda g, i, j, k: (k // ncb, j, k % ncb)),
        pl.BlockSpec((tm, tko), lambda g, i, j, k: (i, j)),
        jax.ShapeDtypeStruct((M, K), F32), (tm, tko))


def mm_tn_raw(at, d, G, tn=None, colperm=_ident, name="mm_tn"):
    K, M = at.shape
    C = d.shape[1] // G
    tko = _pick(K, _TM)
    tn = tn or _pick(C, _TN)
    tm = M if M <= 2048 else _pick(M, (2048, 1024, 512, 256, 128))
    nj = C // tn
    grid = (G, K // tko, nj, M // tm)
    return _mm_call(
        name, at, d, None, NN, grid,
        pl.BlockSpec((tko, tm), lambda g, i, j, k: (i, k)),
        pl.BlockSpec((tm, tn), lambda g, i, j, k: (k, colperm(g * nj + j))),
        pl.BlockSpec((None, tko, tn), lambda g, i, j, k: (g, i, j)),
        jax.ShapeDtypeStruct((G, K, C), BF16), (tko, tn))


def _make_linear(tn, colperm, with_res):
    @jax.custom_vjp
    def lin(a, w, res):
        return mm_nn_raw(a.astype(BF16), w, res if with_res else None, tn=tn, colperm=colperm)

    def fwd(a, w, res):
        a16 = a.astype(BF16)
        return mm_nn_raw(a16, w, res if with_res else None, tn=tn, colperm=colperm), (a16, w)

    def bwd(saved, dout):
        a16, w = saved
        d16 = dout.astype(BF16)
        da = mm_nt_raw(d16, w, tn=tn, colperm=colperm)
        dw = mm_tn_raw(a16.T, d16, w.shape[0], tn=tn, colperm=colperm)
        return da, dw, (dout if with_res else None)

    lin.defvjp(fwd, bwd)
    return lin


def linear(a, w, res=None, tn=None, colperm=_ident):
    if res is None:
        return _make_linear(tn, colperm, False)(a, w, None)
    return _make_linear(tn, colperm, True)(a, w, res)


def _row_tile(R, D):
    return _pick(R, tuple(t for t in (2048, 1024, 512, 256, 128, 64, 32, 16, 8) if t * D * 4 <= (2 << 20)))


def _norm_fwd_call(x, gain, eps, inv_n):
    R, D = x.shape
    tr = _row_tile(R, D)
    has_gain = gain is not None

    def body(*refs):
        if has_gain:
            x_ref, g_ref, y_ref = refs
        else:
            x_ref, y_ref = refs
        xv = x_ref[...]
        r = lax.rsqrt(jnp.sum(xv * xv, axis=-1, keepdims=True) * inv_n + eps)
        y = xv * r
        if has_gain:
            y = y * g_ref[...]
        y_ref[...] = y

    row = pl.BlockSpec((tr, D), lambda i: (i, 0))
    in_specs = [row] + ([pl.BlockSpec((1, D), lambda i: (0, 0))] if has_gain else [])
    args = (x,) + ((gain.reshape(1, D),) if has_gain else ())
    return pl.pallas_call(body, out_shape=jax.ShapeDtypeStruct((R, D), F32), grid=(R // tr,),
                          in_specs=in_specs, out_specs=row, name="norm_fwd",
                          compiler_params=_params(("parallel",)))(*args)


def _norm_bwd_call(x, gain, dy, eps, inv_n):
    R, D = x.shape
    tr = _row_tile(R, D)
    has_gain = gain is not None

    def body(*refs):
        if has_gain:
            x_ref, g_ref, dy_ref, dx_ref, dg_ref = refs
        else:
            x_ref, dy_ref, dx_ref = refs
        xv = x_ref[...]
        dyv = dy_ref[...]
        r = lax.rsqrt(jnp.sum(xv * xv, axis=-1, keepdims=True) * inv_n + eps)
        xh = xv * r
        dxh = dyv * g_ref[...] if has_gain else dyv
        dx_ref[...] = r * (dxh - xh * (inv_n * jnp.sum(dxh * xh, axis=-1, keepdims=True)))
        if has_gain:
            @pl.when(pl.program_id(0) == 0)
            def _():
                dg_ref[...] = jnp.zeros_like(dg_ref)

            dg_ref[...] += jnp.sum(dyv * xh, axis=0, keepdims=True)

    row = pl.BlockSpec((tr, D), lambda i: (i, 0))
    vec = pl.BlockSpec((1, D), lambda i: (0, 0))
    if has_gain:
        dx, dg = pl.pallas_call(
            body, out_shape=(jax.ShapeDtypeStruct((R, D), F32), jax.ShapeDtypeStruct((1, D), F32)),
            grid=(R // tr,), in_specs=[row, vec, row], out_specs=(row, vec), name="norm_bwd",
            compiler_params=_params(("arbitrary",)))(x, gain.reshape(1, D), dy)
        return dx, dg.reshape(D)
    dx = pl.pallas_call(body, out_shape=jax.ShapeDtypeStruct((R, D), F32), grid=(R // tr,),
                        in_specs=[row, row], out_specs=row, name="l2norm_bwd",
                        compiler_params=_params(("parallel",)))(x, dy)
    return dx, None


@functools.partial(jax.custom_vjp, nondiff_argnums=(2,))
def rms_norm(x, gain, eps):
    return _norm_fwd_call(x, gain, eps, 1.0 / x.shape[-1])


def _rms_norm_fwd(x, gain, eps):
    return rms_norm(x, gain, eps), (x, gain)


def _rms_norm_bwd(eps, saved, dy):
    x, gain = saved
    return _norm_bwd_call(x, gain, dy, eps, 1.0 / x.shape[-1])


rms_norm.defvjp(_rms_norm_fwd, _rms_norm_bwd)


@jax.custom_vjp
def l2_norm(x):
    return _norm_fwd_call(x, None, EPS, 1.0)


def _l2_norm_fwd(x):
    return l2_norm(x), x


def _l2_norm_bwd(x, dy):
    return (_norm_bwd_call(x, None, dy, EPS, 1.0)[0],)


l2_norm.defvjp(_l2_norm_fwd, _l2_norm_bwd)


def _group_norm_call(x, gain, dy, groups, eps, mean):
    R, D = x.shape
    gs = D // groups
    tr = _row_tile(R, D)
    inv_n = 1.0 / gs if mean else 1.0
    has_gain = gain is not None
    backward = dy is not None

    def body(*refs):
        refs = list(refs)
        x_ref = refs.pop(0)
        g_ref = refs.pop(0) if has_gain else None
        dy_ref = refs.pop(0) if backward else None
        out_ref = refs.pop(0)
        if backward and has_gain:
            dg_ref = refs.pop(0)

            @pl.when(pl.program_id(0) == 0)
            def _():
                dg_ref[...] = jnp.zeros_like(dg_ref)

        for gi in range(groups):
            cols = slice(gi * gs, (gi + 1) * gs)
            xv = x_ref[:, cols]
            r = lax.rsqrt(jnp.sum(xv * xv, axis=-1, keepdims=True) * inv_n + eps)
            xh = xv * r
            if not backward:
                out_ref[:, cols] = xh * g_ref[...] if has_gain else xh
                continue
            dyv = dy_ref[:, cols]
            dxh = dyv * g_ref[...] if has_gain else dyv
            out_ref[:, cols] = r * (dxh - xh * (inv_n * jnp.sum(dxh * xh, axis=-1, keepdims=True)))
            if has_gain:
                dg_ref[...] += jnp.sum(dyv * xh, axis=0, keepdims=True)

    row = pl.BlockSpec((tr, D), lambda i: (i, 0))
    vec = pl.BlockSpec((1, gs), lambda i: (0, 0))
    in_specs = [row] + ([vec] if has_gain else []) + ([row] if backward else [])
    args = (x,) + ((gain.reshape(1, gs),) if has_gain else ()) + ((dy,) if backward else ())
    full = jax.ShapeDtypeStruct((R, D), F32)
    if backward and has_gain:
        dx, dg = pl.pallas_call(body, out_shape=(full, jax.ShapeDtypeStruct((1, gs), F32)), grid=(R // tr,),
                                in_specs=in_specs, out_specs=(row, vec), name="group_norm_bwd",
                                compiler_params=_params(("arbitrary",)))(*args)
        return dx, dg.reshape(gs)
    out = pl.pallas_call(body, out_shape=full, grid=(R // tr,), in_specs=in_specs, out_specs=row,
                         name="group_norm_bwd" if backward else "group_norm_fwd",
                         compiler_params=_params(("parallel",)))(*args)
    return (out, None) if backward else out


@functools.partial(jax.custom_vjp, nondiff_argnums=(2, 3))
def group_rms_norm(x, gain, groups, eps):
    return _group_norm_call(x, gain, None, groups, eps, True)


def _group_rms_norm_fwd(x, gain, groups, eps):
    return group_rms_norm(x, gain, groups, eps), (x, gain)


def _group_rms_norm_bwd(groups, eps, saved, dy):
    x, gain = saved
    return _group_norm_call(x, gain, dy, groups, eps, True)


group_rms_norm.defvjp(_group_rms_norm_fwd, _group_rms_norm_bwd)


@functools.partial(jax.custom_vjp, nondiff_argnums=(1,))
def group_l2_norm(x, groups):
    return _group_norm_call(x, None, None, groups, EPS, False)


def _group_l2_norm_fwd(x, groups):
    return group_l2_norm(x, groups), x


def _group_l2_norm_bwd(groups, x, dy):
    return (_group_norm_call(x, None, dy, groups, EPS, False)[0],)


group_l2_norm.defvjp(_group_l2_norm_fwd, _group_l2_norm_bwd)


def _attn_mask(tq, sk):
    q0 = pl.program_id(1) * tq
    qc = (q0 + lax.broadcasted_iota(jnp.int32, (tq, sk), 0)) // CHUNK
    kc = lax.broadcasted_iota(jnp.int32, (tq, sk), 1) // CHUNK
    return kc <= qc


def _attn_dims(q, k, v, heads):
    if heads is None:
        H, S, dk = q.shape
        return H, S, dk, k.shape[1], v.shape[2]
    return heads, q.shape[0], q.shape[1] // heads, k.shape[0], v.shape[1] // heads


def _attn_specs(heads, tq, dk, Sk, dv):
    if heads is None:
        return (pl.BlockSpec((None, tq, dk), lambda h, i: (h, i, 0)), pl.BlockSpec((None, Sk, dk), lambda h, i: (h, 0, 0)),
                pl.BlockSpec((None, Sk, dv), lambda h, i: (h, 0, 0)), pl.BlockSpec((None, tq, dv), lambda h, i: (h, i, 0)))
    return (pl.BlockSpec((tq, dk), lambda h, i: (i, h)), pl.BlockSpec((Sk, dk), lambda h, i: (0, h)),
            pl.BlockSpec((Sk, dv), lambda h, i: (0, h)), pl.BlockSpec((tq, dv), lambda h, i: (i, h)))


def _attn_fwd_call(q, k, v, scale, causal, heads):
    H, S, dk, Sk, dv = _attn_dims(q, k, v, heads)
    tq = _pick(S, (256, 128, 64))
    qs, ks, vs, os_ = _attn_specs(heads, tq, dk, Sk, dv)

    def body(q_ref, k_ref, v_ref, o_ref, lse_ref):
        s = lax.dot_general(q_ref[...].astype(BF16), k_ref[...].astype(BF16), NT,
                            preferred_element_type=F32) * scale
        if causal:
            s = jnp.where(_attn_mask(tq, Sk), s, -1e30)
        m = jnp.max(s, axis=-1, keepdims=True)
        p = jnp.exp(s - m)
        l = jnp.sum(p, axis=-1, keepdims=True)
        o = lax.dot_general(p.astype(BF16), v_ref[...].astype(BF16), NN, preferred_element_type=F32)
        o_ref[...] = o / l
        lse_ref[...] = m + jnp.log(l)

    o_shape = (H, S, dv) if heads is None else (S, H * dv)
    return pl.pallas_call(
        body,
        out_shape=(jax.ShapeDtypeStruct(o_shape, F32), jax.ShapeDtypeStruct((H, S, 1), F32)),
        grid=(H, S // tq), in_specs=[qs, ks, vs],
        out_specs=(os_, pl.BlockSpec((None, tq, 1), lambda h, i: (h, i, 0))),
        name="attn_fwd", compiler_params=_params(("parallel", "parallel")))(q, k, v)


def _attn_bwd_call(q, k, v, o, lse, do, scale, causal, heads):
    H, S, dk, Sk, dv = _attn_dims(q, k, v, heads)
    tq = _pick(S, (256, 128, 64))
    qs, ks, vs, os_ = _attn_specs(heads, tq, dk, Sk, dv)

    def body(q_ref, k_ref, v_ref, o_ref, lse_ref, do_ref, dq_ref, dk_ref, dv_ref):
        qb = q_ref[...].astype(BF16)
        kb = k_ref[...].astype(BF16)
        dob = do_ref[...].astype(BF16)
        s = lax.dot_general(qb, kb, NT, preferred_element_type=F32) * scale
        if causal:
            s = jnp.where(_attn_mask(tq, Sk), s, -1e30)
        p = jnp.exp(s - lse_ref[...])
        dp = lax.dot_general(dob, v_ref[...].astype(BF16), NT, preferred_element_type=F32)
        delta = jnp.sum(do_ref[...] * o_ref[...], axis=-1, keepdims=True)
        ds = (p * (dp - delta) * scale).astype(BF16)
        dq_ref[...] = lax.dot_general(ds, kb, NN, preferred_element_type=F32)

        @pl.when(pl.program_id(1) == 0)
        def _():
            dk_ref[...] = jnp.zeros_like(dk_ref)
            dv_ref[...] = jnp.zeros_like(dv_ref)

        dk_ref[...] += lax.dot_general(ds, qb, TN, preferred_element_type=F32)
        dv_ref[...] += lax.dot_general(p.astype(BF16), dob, TN, preferred_element_type=F32)

    ls = pl.BlockSpec((None, tq, 1), lambda h, i: (h, i, 0))
    return pl.pallas_call(
        body,
        out_shape=(jax.ShapeDtypeStruct(q.shape, F32), jax.ShapeDtypeStruct(k.shape, F32),
                   jax.ShapeDtypeStruct(v.shape, F32)),
        grid=(H, S // tq), in_specs=[qs, ks, vs, os_, ls, os_], out_specs=(qs, ks, vs),
        name="attn_bwd", compiler_params=_params(("parallel", "arbitrary")))(q, k, v, o, lse, do)


@functools.partial(jax.custom_vjp, nondiff_argnums=(3, 4, 5))
def attention(q, k, v, scale, causal, heads=None):
    return _attn_fwd_call(q, k, v, scale, causal, heads)[0]


def _attention_fwd(q, k, v, scale, causal, heads):
    o, lse = _attn_fwd_call(q, k, v, scale, causal, heads)
    return o, (q, k, v, o, lse)


def _attention_bwd(scale, causal, heads, saved, do):
    q, k, v, o, lse = saved
    return _attn_bwd_call(q, k, v, o, lse, do, scale, causal, heads)


attention.defvjp(_attention_fwd, _attention_bwd)


NNB = (((2,), (1,)), ((0,), (0,)))
NTB = (((2,), (2,)), ((0,), (0,)))
TNB = (((1,), (1,)), ((0,), (0,)))


def _dot3(a, b, dims):
    a_hi, b_hi = a.astype(BF16), b.astype(BF16)
    a_lo = (a - a_hi.astype(F32)).astype(BF16)
    b_lo = (b - b_hi.astype(F32)).astype(BF16)

    def dot(x, y):
        return lax.dot_general(x, y, dims, preferred_element_type=F32)

    return dot(a_hi, b_hi) + (dot(a_hi, b_lo) + dot(a_lo, b_hi))


@functools.partial(jax.custom_vjp, nondiff_argnums=(2,))
def _hdot(a, b, dims):
    return _dot3(a, b, dims)


def _hdot_fwd(a, b, dims):
    return _dot3(a, b, dims), (a, b)


def _hdot_bwd(dims, saved, g):
    a, b = saved
    if dims == NNB:
        return _dot3(g, b, NTB), _dot3(a, g, TNB)
    if dims == NTB:
        return _dot3(g, b, NNB), _dot3(g, a, TNB)
    assert dims == TNB
    return _dot3(b, g, NTB), _dot3(a, g, NNB)


_hdot.defvjp(_hdot_fwd, _hdot_bwd)


def _dn_chunk(q, k, v, gcol, grow, bcol, state):
    c = q.shape[1]
    r_i = lax.broadcasted_iota(jnp.int32, (1, c, c), 1)
    c_i = lax.broadcasted_iota(jnp.int32, (1, c, c), 2)
    incl = c_i <= r_i
    strict = c_i < r_i
    g_cum_col = jnp.sum(jnp.where(incl, grow, 0.0), axis=2, keepdims=True)
    g_cum_row = jnp.sum(jnp.where(r_i <= c_i, gcol, 0.0), axis=1, keepdims=True)
    g_last = jnp.sum(grow, axis=2, keepdims=True)
    decay = jnp.where(incl, jnp.exp(jnp.where(incl, g_cum_col - g_cum_row, 0.0)), 0.0)
    qs = q * (q.shape[2] ** -0.5)
    kb = k * bcol
    x = -jnp.where(strict, _hdot(kb, k, NTB) * decay, 0.0)
    eye = (r_i == c_i).astype(F32)
    t = eye + x
    steps = max(1, int(np.ceil(np.log2(c))) - 1)
    for _ in range(steps):
        x = _hdot(x, x, NNB)
        t = t + _hdot(t, x, NNB)
    e_col = jnp.exp(g_cum_col)
    u = _hdot(t, v * bcol, NNB)
    w = _hdot(t, kb * e_col, NNB)
    attn = _hdot(qs, k, NTB) * decay
    v_new = u - _hdot(w, state, NNB)
    o = _hdot(qs * e_col, state, NNB) + _hdot(attn, v_new, NNB)
    k_dec = k * jnp.exp(g_last - g_cum_col)
    new_state = state * jnp.exp(g_last) + _hdot(k_dec, v_new, TNB)
    return o, new_state


DN_HEADS_PER_STEP = 8


def _dn_specs(hb, D, chunk_index):
    seq = pl.BlockSpec((CHUNK, hb * D), lambda h, n: (chunk_index(n), h))
    col = pl.BlockSpec((hb, CHUNK, 1), lambda h, n: (h, chunk_index(n), 0))
    row = pl.BlockSpec((hb, None, 1, CHUNK), lambda h, n: (h, chunk_index(n), 0, 0))
    st = pl.BlockSpec((hb, None, D, D), lambda h, n: (h, chunk_index(n), 0, 0))
    return seq, col, row, st


def _dn_heads(ref, hb, D):
    return jnp.stack([ref[:, h * D:(h + 1) * D] for h in range(hb)])


def _dn_fwd_call(q, k, v, gcol, grow, bcol):
    H = gcol.shape[0]
    S, D = q.shape[0], q.shape[1] // H
    N = S // CHUNK
    hb = _pick(H, (DN_HEADS_PER_STEP, 2, 1))
    seq, col, row, st = _dn_specs(hb, D, lambda n: n)

    def body(q_ref, k_ref, v_ref, gc_ref, gr_ref, bc_ref, o_ref, st_ref, state):
        @pl.when(pl.program_id(1) == 0)
        def _():
            state[...] = jnp.zeros_like(state)

        s_in = state[...]
        st_ref[...] = s_in
        o, s_out = _dn_chunk(_dn_heads(q_ref, hb, D), _dn_heads(k_ref, hb, D), _dn_heads(v_ref, hb, D),
                             gc_ref[...], gr_ref[...], bc_ref[...], s_in)
        for h in range(hb):
            o_ref[:, h * D:(h + 1) * D] = o[h]
        state[...] = s_out

    return pl.pallas_call(
        body,
        out_shape=(jax.ShapeDtypeStruct((S, H * D), F32), jax.ShapeDtypeStruct((H, N, D, D), F32)),
        grid=(H // hb, N), in_specs=[seq, seq, seq, col, row, col], out_specs=(seq, st),
        scratch_shapes=[pltpu.VMEM((hb, D, D), F32)], name="deltanet_fwd",
        compiler_params=_params(("parallel", "arbitrary")))(q, k, v, gcol, grow, bcol)


def _dn_bwd_call(q, k, v, gcol, grow, bcol, states, do):
    H = gcol.shape[0]
    S, D = q.shape[0], q.shape[1] // H
    N = S // CHUNK
    hb = _pick(H, (DN_HEADS_PER_STEP, 2, 1))
    seq, col, row, st = _dn_specs(hb, D, lambda n: N - 1 - n)

    def body(q_ref, k_ref, v_ref, gc_ref, gr_ref, bc_ref, st_ref, do_ref,
             dq_ref, dk_ref, dv_ref, dgc_ref, dgr_ref, dbc_ref, dstate):
        @pl.when(pl.program_id(1) == 0)
        def _():
            dstate[...] = jnp.zeros_like(dstate)

        _, vjp = jax.vjp(_dn_chunk, _dn_heads(q_ref, hb, D), _dn_heads(k_ref, hb, D), _dn_heads(v_ref, hb, D),
                         gc_ref[...], gr_ref[...], bc_ref[...], st_ref[...])
        grads = vjp((_dn_heads(do_ref, hb, D), dstate[...]))
        for ref, val in zip((dq_ref, dk_ref, dv_ref), grads[:3]):
            for h in range(hb):
                ref[:, h * D:(h + 1) * D] = val[h]
        for ref, val in zip((dgc_ref, dgr_ref, dbc_ref, dstate), grads[3:]):
            ref[...] = val

    sd = jax.ShapeDtypeStruct
    return pl.pallas_call(
        body,
        out_shape=(sd((S, H * D), F32), sd((S, H * D), F32), sd((S, H * D), F32),
                   sd((H, S, 1), F32), sd((H, N, 1, CHUNK), F32), sd((H, S, 1), F32)),
        grid=(H // hb, N), in_specs=[seq, seq, seq, col, row, col, st, seq],
        out_specs=(seq, seq, seq, col, row, col),
        scratch_shapes=[pltpu.VMEM((hb, D, D), F32)], name="deltanet_bwd",
        compiler_params=_params(("parallel", "arbitrary")))(q, k, v, gcol, grow, bcol, states, do)


@jax.custom_vjp
def delta_rule(q, k, v, gcol, grow, bcol):
    return _dn_fwd_call(q, k, v, gcol, grow, bcol)[0]


def _delta_rule_fwd(q, k, v, gcol, grow, bcol):
    o, states = _dn_fwd_call(q, k, v, gcol, grow, bcol)
    return o, (q, k, v, gcol, grow, bcol, states)


def _delta_rule_bwd(saved, do):
    return _dn_bwd_call(*saved, do)


delta_rule.defvjp(_delta_rule_fwd, _delta_rule_bwd)


def _shift_down(x, j):
    if j == 0:
        return x
    rows = lax.broadcasted_iota(jnp.int32, x.shape, 0)
    return jnp.where(rows >= j, pltpu.roll(x, j, 0), 0.0)


def _shift_up(x, j):
    if j == 0:
        return x
    s = x.shape[0]
    rows = lax.broadcasted_iota(jnp.int32, x.shape, 0)
    return jnp.where(rows < s - j, pltpu.roll(x, s - j, 0), 0.0)


def _conv(x, w):
    kw = w.shape[0]
    acc = x * w[kw - 1:kw, :]
    for kk in range(kw - 1):
        acc = acc + _shift_down(x, kw - 1 - kk) * w[kk:kk + 1, :]
    return acc


def _conv_t(d, w):
    kw = w.shape[0]
    acc = d * w[kw - 1:kw, :]
    for kk in range(kw - 1):
        acc = acc + _shift_up(d, kw - 1 - kk) * w[kk:kk + 1, :]
    return acc


def _conv_dw_rows(d, x, kw):
    return [jnp.sum(d * _shift_down(x, kw - 1 - kk), axis=0, keepdims=True) for kk in range(kw)]


def _silu(a):
    return a * jax.nn.sigmoid(a)


def _dsilu(a):
    s = jax.nn.sigmoid(a)
    return s * (1.0 + a * (1.0 - s))


def _conv_silu_fwd_call(x, w):
    S, C = x.shape
    kw = w.shape[0]
    tc = _pick(C, (256, 128))

    def body(x_ref, w_ref, y_ref):
        y_ref[...] = _silu(_conv(x_ref[...], w_ref[...]))

    xs = pl.BlockSpec((S, tc), lambda j: (0, j))
    ws = pl.BlockSpec((kw, tc), lambda j: (0, j))
    return pl.pallas_call(body, out_shape=jax.ShapeDtypeStruct((S, C), F32), grid=(C // tc,),
                          in_specs=[xs, ws], out_specs=xs, name="conv_silu_fwd",
                          compiler_params=_params(("parallel",)))(x, w)


def _conv_silu_bwd_call(x, w, dy):
    S, C = x.shape
    kw = w.shape[0]
    tc = _pick(C, (256, 128))

    def body(x_ref, w_ref, dy_ref, dx_ref, dw_ref):
        xv = x_ref[...]
        wv = w_ref[...]
        da = dy_ref[...] * _dsilu(_conv(xv, wv))
        dx_ref[...] = _conv_t(da, wv)
        for kk, row in enumerate(_conv_dw_rows(da, xv, kw)):
            dw_ref[kk:kk + 1, :] = row

    xs = pl.BlockSpec((S, tc), lambda j: (0, j))
    ws = pl.BlockSpec((kw, tc), lambda j: (0, j))
    return pl.pallas_call(
        body, out_shape=(jax.ShapeDtypeStruct((S, C), F32), jax.ShapeDtypeStruct((kw, C), F32)),
        grid=(C // tc,), in_specs=[xs, ws, xs], out_specs=(xs, ws), name="conv_silu_bwd",
        compiler_params=_params(("parallel",)))(x, w, dy)


@jax.custom_vjp
def conv_silu(x, w):
    return _conv_silu_fwd_call(x, w)


def _conv_silu_fwd(x, w):
    return conv_silu(x, w), (x, w)


def _conv_silu_bwd(saved, dy):
    return _conv_silu_bwd_call(*saved, dy)


conv_silu.defvjp(_conv_silu_fwd, _conv_silu_bwd)


def _conv_glu_fwd_call(x, w, b, tc):
    S, C2 = x.shape
    kw = w.shape[0]
    nb = C2 // (2 * tc)

    def body(x_ref, w_ref, b_ref, y_ref):
        a = _conv(x_ref[...], w_ref[...]) + b_ref[...]
        y_ref[...] = _silu(a[:, :tc]) * a[:, tc:]

    return pl.pallas_call(
        body, out_shape=jax.ShapeDtypeStruct((S, C2 // 2), F32), grid=(nb,),
        in_specs=[pl.BlockSpec((S, 2 * tc), lambda j: (0, j)), pl.BlockSpec((kw, 2 * tc), lambda j: (0, j)),
                  pl.BlockSpec((1, 2 * tc), lambda j: (0, j))],
        out_specs=pl.BlockSpec((S, tc), lambda j: (0, j)), name="conv_glu_fwd",
        compiler_params=_params(("parallel",)))(x, w, b)


def _conv_glu_bwd_call(x, w, b, dy, tc):
    S, C2 = x.shape
    kw = w.shape[0]
    nb = C2 // (2 * tc)

    def body(x_ref, w_ref, b_ref, dy_ref, dx_ref, dw_ref, db_ref):
        xv = x_ref[...]
        wv = w_ref[...]
        a = _conv(xv, wv) + b_ref[...]
        ag, au = a[:, :tc], a[:, tc:]
        dyv = dy_ref[...]
        da = jnp.concatenate([dyv * au * _dsilu(ag), dyv * _silu(ag)], axis=1)
        dx_ref[...] = _conv_t(da, wv)
        for kk, row in enumerate(_conv_dw_rows(da, xv, kw)):
            dw_ref[kk:kk + 1, :] = row
        db_ref[...] = jnp.sum(da, axis=0, keepdims=True)

    xs = pl.BlockSpec((S, 2 * tc), lambda j: (0, j))
    ws = pl.BlockSpec((kw, 2 * tc), lambda j: (0, j))
    bs = pl.BlockSpec((1, 2 * tc), lambda j: (0, j))
    return pl.pallas_call(
        body, out_shape=(jax.ShapeDtypeStruct((S, C2), F32), jax.ShapeDtypeStruct((kw, C2), F32),
                         jax.ShapeDtypeStruct((1, C2), F32)),
        grid=(nb,), in_specs=[xs, ws, bs, pl.BlockSpec((S, tc), lambda j: (0, j))], out_specs=(xs, ws, bs),
        name="conv_glu_bwd", compiler_params=_params(("parallel",)))(x, w, b, dy)


@functools.partial(jax.custom_vjp, nondiff_argnums=(3,))
def conv_glu(x, w, b, tc):
    return _conv_glu_fwd_call(x, w, b, tc)


def _conv_glu_fwd(x, w, b, tc):
    return conv_glu(x, w, b, tc), (x, w, b)


def _conv_glu_bwd(tc, saved, dy):
    return _conv_glu_bwd_call(*saved, dy, tc)


conv_glu.defvjp(_conv_glu_fwd, _conv_glu_bwd)


def _sqerr_call(y, t):
    S, D = y.shape
    tr = _row_tile(S, D)

    def body(y_ref, t_ref, l_ref, dy_ref):
        d = y_ref[...] - t_ref[...]
        dy_ref[...] = d * (1.0 / D)

        @pl.when(pl.program_id(0) == 0)
        def _():
            l_ref[...] = jnp.zeros_like(l_ref)

        l_ref[...] += jnp.full((8, LANES), 0.5 / D, F32) * jnp.sum(d * d)

    row = pl.BlockSpec((tr, D), lambda i: (i, 0))
    one = pl.BlockSpec((8, LANES), lambda i: (0, 0))
    return pl.pallas_call(
        body, out_shape=(jax.ShapeDtypeStruct((8, LANES), F32), jax.ShapeDtypeStruct((S, D), F32)),
        grid=(S // tr,), in_specs=[row, row], out_specs=(one, row), name="loss_head",
        compiler_params=_params(("arbitrary",)))(y, t)


@jax.custom_vjp
def sq_loss(y, t):
    return _sqerr_call(y, t)[0][0, 0]


def _sq_loss_fwd(y, t):
    l, dy = _sqerr_call(y, t)
    return l[0, 0], dy


def _sq_loss_bwd(dy, dl):
    return dy * dl, None


sq_loss.defvjp(_sq_loss_fwd, _sq_loss_bwd)


def adamw(w, g, m, v):
    shape = w.shape
    C = shape[-1]
    R = int(np.prod(shape[:-1]))
    tr = _pick(R, tuple(t for t in (4096, 2048, 1024, 512, 256, 128, 64, 32, 16, 8) if t * C * 4 <= (1 << 20)))
    c1 = 1.0 / (1.0 - ADAM_B1 ** ADAM_STEP)
    c2 = 1.0 / (1.0 - ADAM_B2 ** ADAM_STEP)

    def body(w_ref, g_ref, m_ref, v_ref, d_ref, nm_ref, nv_ref):
        gv = g_ref[...]
        nm = ADAM_B1 * m_ref[...] + (1.0 - ADAM_B1) * gv
        nv = ADAM_B2 * v_ref[...] + (1.0 - ADAM_B2) * (gv * gv)
        d_ref[...] = -ADAM_LR * ((nm * c1) / (jnp.sqrt(nv * c2) + ADAM_EPS) + ADAM_WD * w_ref[...])
        nm_ref[...] = nm
        nv_ref[...] = nv

    blk = pl.BlockSpec((tr, C), lambda i: (i, 0))
    out = jax.ShapeDtypeStruct((R, C), F32)
    d, nm, nv = pl.pallas_call(body, out_shape=(out, out, out), grid=(R // tr,), in_specs=[blk] * 4,
                               out_specs=(blk, blk, blk), name="adamw",
                               compiler_params=_params(("parallel",)))(
        w.reshape(R, C), g.reshape(R, C), m.reshape(R, C), v.reshape(R, C))
    return d.reshape(shape), nm.reshape(shape), nv.reshape(shape)


def _place():
    x, y, c = lax.axis_index("x"), lax.axis_index("y"), lax.axis_index("c")
    chips = [(1 - x, y), (x, 1 - y), (1 - x, 1 - y)]
    return x, y, c, chips


def small_allgather(buf):
    m_per, n = buf.shape

    def body(x_ref, out_ref, send_sems, recv_sems, local_sem):
        x, y, c, chips = _place()
        me, sibling = (x, y, c), (x, y, 1 - c)

        def rows(px, py, pc):
            return out_ref.at[4 * px + 2 * py + pc]

        def copy(k, block, to, src=None):
            return pltpu.make_async_remote_copy(
                src_ref=rows(*block) if src is None else src, dst_ref=rows(*block),
                send_sem=send_sems.at[k], recv_sem=recv_sems.at[k], device_id=to, device_id_type=MESH)

        mine = pltpu.make_async_copy(x_ref, rows(*me), local_sem)
        mine.start()
        first = [copy(0, me, sibling, src=x_ref)]
        first += [copy(1 + j, me, (*chip, c), src=x_ref) for j, chip in enumerate(chips)]
        for cp in first:
            cp.start()
        passed = [copy(4 + j, (*chip, c), sibling) for j, chip in enumerate(chips)]
        for j, chip in enumerate(chips):
            copy(1 + j, (*chip, c), me).wait_recv()
            passed[j].start()
        copy(0, sibling, me).wait_recv()
        for j, chip in enumerate(chips):
            copy(4 + j, (*chip, 1 - c), me).wait_recv()
        for cp in first + passed:
            cp.wait_send()
        mine.wait()

    return pl.pallas_call(
        body, out_shape=jax.ShapeDtypeStruct((N_DEV, m_per, n), buf.dtype),
        in_specs=[pl.BlockSpec(memory_space=pltpu.VMEM)], out_specs=pl.BlockSpec(memory_space=pltpu.VMEM),
        scratch_shapes=[pltpu.SemaphoreType.DMA((7,)), pltpu.SemaphoreType.DMA((7,)), pltpu.SemaphoreType.DMA],
        name="small_allgather", compiler_params=pltpu.CompilerParams(vmem_limit_bytes=VMEM_LIMIT))(buf)


def sum_blocks(g):
    n, R, C = g.shape
    tr = _pick(R, (512, 256, 128, 64, 32, 16, 8))

    def body(g_ref, o_ref):
        acc = g_ref[0].astype(F32)
        for d in range(1, n):
            acc = acc + g_ref[d].astype(F32)
        o_ref[...] = acc

    return pl.pallas_call(body, out_shape=jax.ShapeDtypeStruct((R, C), F32), grid=(R // tr,),
                          in_specs=[pl.BlockSpec((n, tr, C), lambda i: (0, i, 0))],
                          out_specs=pl.BlockSpec((tr, C), lambda i: (i, 0)), name="sum_blocks",
                          compiler_params=_params(("parallel",)))(g)


def _hbm_specs(n):
    return [pl.BlockSpec(memory_space=pl.ANY)] * n


def gather_weights(shards):
    n = len(shards)

    def body(*refs):
        src, out = refs[:n], refs[n:2 * n]
        send_sems, recv_sems = refs[2 * n:]
        x, y, c, chips = _place()
        xn, yn, dg = chips
        sibling = (x, y, 1 - c)
        my_chip = 2 * x + y
        slot = lambda chip: 2 * chip[0] + chip[1]
        started = []

        def copy(t, k, ref, to):
            return pltpu.make_async_remote_copy(
                src_ref=ref, dst_ref=ref, send_sem=send_sems.at[t, k], recv_sem=recv_sems.at[t, k],
                device_id=to, device_id_type=MESH)

        def start(cp):
            cp.start()
            started.append(cp)

        def rows(t, chip, first, count):
            return out[t].at[slot(chip), pl.ds(first, count)]

        for t in range(n):
            half = src[t].shape[0] // 2
            for k, chip in enumerate((xn, yn)):
                cp = pltpu.make_async_remote_copy(
                    src_ref=src[t].at[pl.ds(c * half, half)], dst_ref=out[t].at[my_chip, pl.ds(c * half, half)],
                    send_sem=send_sems.at[t, k], recv_sem=recv_sems.at[t, k], device_id=(*chip, c), device_id_type=MESH)
                start(cp)
        for t in range(n):
            half = src[t].shape[0] // 2
            quarter = half // 2
            copy(t, 0, rows(t, xn, c * half, half), (*xn, c)).wait_recv()
            start(copy(t, 2, rows(t, xn, c * half, quarter), (*yn, c)))
            start(copy(t, 4, rows(t, xn, c * half, half), sibling))
            copy(t, 1, rows(t, yn, c * half, half), (*yn, c)).wait_recv()
            start(copy(t, 3, rows(t, yn, c * half + quarter, quarter), (*xn, c)))
            start(copy(t, 5, rows(t, yn, c * half, half), sibling))
        for t in range(n):
            half = src[t].shape[0] // 2
            quarter = half // 2
            copy(t, 2, rows(t, dg, c * half, quarter), (*yn, c)).wait_recv()
            copy(t, 3, rows(t, dg, c * half + quarter, quarter), (*xn, c)).wait_recv()
            start(copy(t, 6, rows(t, dg, c * half, half), sibling))
        for t in range(n):
            half = src[t].shape[0] // 2
            for k, chip in ((4, xn), (5, yn), (6, dg)):
                copy(t, k, rows(t, chip, (1 - c) * half, half), sibling).wait_recv()
        for cp in started:
            cp.wait_send()

    out_shape = tuple(jax.ShapeDtypeStruct((N_CHIPS,) + s.shape, s.dtype) for s in shards)
    return pl.pallas_call(
        body, out_shape=out_shape, in_specs=_hbm_specs(n), out_specs=tuple(_hbm_specs(n)),
        scratch_shapes=[pltpu.SemaphoreType.DMA((n, 7)), pltpu.SemaphoreType.DMA((n, 7))],
        name="gather_weights", compiler_params=pltpu.CompilerParams(has_side_effects=True))(*shards)


def sibling_exchange_halves(grads):
    n = len(grads)

    def body(*refs):
        src, out = refs[:n], refs[n:2 * n]
        send_sems, recv_sems = refs[2 * n:]
        x, y, c, _ = _place()
        sibling = (x, y, 1 - c)
        cps = []
        for t in range(n):
            half = src[t].shape[1] // 2
            cp = pltpu.make_async_remote_copy(
                src_ref=src[t].at[:, pl.ds((1 - c) * half, half)], dst_ref=out[t],
                send_sem=send_sems.at[t], recv_sem=recv_sems.at[t], device_id=sibling, device_id_type=MESH)
            cp.start()
            cps.append(cp)
        for cp in cps:
            cp.wait()

    out_shape = tuple(jax.ShapeDtypeStruct((s.shape[0], s.shape[1] // 2, s.shape[2]), s.dtype) for s in grads)
    return pl.pallas_call(
        body, out_shape=out_shape, in_specs=_hbm_specs(n), out_specs=tuple(_hbm_specs(n)),
        scratch_shapes=[pltpu.SemaphoreType.DMA((n,)), pltpu.SemaphoreType.DMA((n,))],
        name="sibling_exchange_halves", compiler_params=pltpu.CompilerParams(has_side_effects=True))(*grads)


def pair_sum(g, recv, c_idx):
    G, R, C = g.shape
    half = R // 2
    tr = _pick(half, tuple(t for t in (1024, 512, 256, 128, 64, 32, 16) if t * C * 2 <= (4 << 20)))
    g4 = g.reshape(G, 2, half, C)

    def body(c_ref, g_ref, r_ref, o_ref):
        o_ref[...] = (g_ref[...].astype(F32) + r_ref[...].astype(F32)).astype(o_ref.dtype)

    grid_spec = pltpu.PrefetchScalarGridSpec(
        num_scalar_prefetch=1, grid=(G, half // tr),
        in_specs=[pl.BlockSpec((None, None, tr, C), lambda i, j, c_ref: (i, c_ref[0], j, 0)),
                  pl.BlockSpec((None, tr, C), lambda i, j, c_ref: (i, j, 0))],
        out_specs=pl.BlockSpec((None, tr, C), lambda i, j, c_ref: (i, j, 0)))
    return pl.pallas_call(body, out_shape=jax.ShapeDtypeStruct((G, half, C), g.dtype), grid_spec=grid_spec,
                          name="pair_sum", compiler_params=_params(("parallel", "parallel")))(c_idx, g4, recv)


def chip_exchange(parts):
    n = len(parts)

    def body(*refs):
        src, out = refs[:n], refs[n:2 * n]
        send_sems, recv_sems = refs[2 * n:]
        x, y, c, chips = _place()
        cps = []
        for t in range(n):
            for j, chip in enumerate(chips):
                cp = pltpu.make_async_remote_copy(
                    src_ref=src[t].at[2 * chip[0] + chip[1]], dst_ref=out[t].at[j],
                    send_sem=send_sems.at[t, j], recv_sem=recv_sems.at[t, j],
                    device_id=(*chip, c), device_id_type=MESH)
                cp.start()
                cps.append(cp)
        for cp in cps:
            cp.wait()

    out_shape = tuple(jax.ShapeDtypeStruct((3,) + s.shape[1:], s.dtype) for s in parts)
    return pl.pallas_call(
        body, out_shape=out_shape, in_specs=_hbm_specs(n), out_specs=tuple(_hbm_specs(n)),
        scratch_shapes=[pltpu.SemaphoreType.DMA((n, 3)), pltpu.SemaphoreType.DMA((n, 3))],
        name="chip_exchange", compiler_params=pltpu.CompilerParams(has_side_effects=True))(*parts)


def sum_slots_into(parts, recv, chip_idx, acc, layer, depth):
    _, H, C = parts.shape
    tr = _pick(H, tuple(t for t in (1024, 512, 256, 128, 64, 32, 16) if t * C * 4 <= (1 << 20)))
    has_acc = acc is not None

    def body(chip_ref, p_ref, r_ref, *rest):
        o_ref = rest[-1]
        o_ref[...] = ((r_ref[0].astype(F32) + r_ref[1].astype(F32)) + r_ref[2].astype(F32)) + p_ref[...].astype(F32)

    in_specs = [pl.BlockSpec((None, tr, C), lambda i, chip_ref: (chip_ref[0], i, 0)),
                pl.BlockSpec((3, tr, C), lambda i, chip_ref: (0, i, 0))]
    args = [chip_idx, parts, recv]
    if has_acc:
        in_specs.append(pl.BlockSpec(memory_space=pl.ANY))
        args.append(acc)
    grid_spec = pltpu.PrefetchScalarGridSpec(
        num_scalar_prefetch=1, grid=(H // tr,), in_specs=in_specs,
        out_specs=pl.BlockSpec((None, tr, C), lambda i, chip_ref: (layer, i, 0)))
    return pl.pallas_call(body, out_shape=jax.ShapeDtypeStruct((depth, H, C), F32), grid_spec=grid_spec,
                          input_output_aliases=({3: 0} if has_acc else {}), name="sum_slots",
                          compiler_params=_params(("parallel",)))(*args)


def sibling_swap(mine):
    n = len(mine)

    def body(*refs):
        src, out = refs[:n], refs[n:2 * n]
        send_sems, recv_sems = refs[2 * n:]
        x, y, c, _ = _place()
        cps = []
        for t in range(n):
            cp = pltpu.make_async_remote_copy(
                src_ref=src[t], dst_ref=out[t], send_sem=send_sems.at[t], recv_sem=recv_sems.at[t],
                device_id=(x, y, 1 - c), device_id_type=MESH)
            cp.start()
            cps.append(cp)
        for cp in cps:
            cp.wait()

    out_shape = tuple(jax.ShapeDtypeStruct(s.shape, s.dtype) for s in mine)
    return pl.pallas_call(
        body, out_shape=out_shape, in_specs=_hbm_specs(n), out_specs=tuple(_hbm_specs(n)),
        scratch_shapes=[pltpu.SemaphoreType.DMA((n,)), pltpu.SemaphoreType.DMA((n,))],
        name="sibling_swap", compiler_params=pltpu.CompilerParams(has_side_effects=True))(*mine)


def chip_and_sibling_exchange(parts, grads):
    n, m = len(parts), len(grads)

    def body(*refs):
        src, nxt = refs[:n], refs[n:n + m]
        out, rcv = refs[n + m:2 * n + m], refs[2 * n + m:2 * (n + m)]
        send_sems, recv_sems, d2d_send, d2d_recv = refs[2 * (n + m):]
        x, y, c, chips = _place()
        cps = []
        for t in range(n):
            for j, chip in enumerate(chips):
                cp = pltpu.make_async_remote_copy(
                    src_ref=src[t].at[2 * chip[0] + chip[1]], dst_ref=out[t].at[j],
                    send_sem=send_sems.at[t, j], recv_sem=recv_sems.at[t, j],
                    device_id=(*chip, c), device_id_type=MESH)
                cp.start()
                cps.append(cp)
        for t in range(m):
            half = nxt[t].shape[1] // 2
            cp = pltpu.make_async_remote_copy(
                src_ref=nxt[t].at[:, pl.ds((1 - c) * half, half)], dst_ref=rcv[t],
                send_sem=d2d_send.at[t], recv_sem=d2d_recv.at[t], device_id=(x, y, 1 - c), device_id_type=MESH)
            cp.start()
            cps.append(cp)
        for cp in cps:
            cp.wait()

    out_shape = (tuple(jax.ShapeDtypeStruct((3,) + s.shape[1:], s.dtype) for s in parts)
                 + tuple(jax.ShapeDtypeStruct((s.shape[0], s.shape[1] // 2, s.shape[2]), s.dtype) for s in grads))
    res = pl.pallas_call(
        body, out_shape=out_shape, in_specs=_hbm_specs(n + m), out_specs=tuple(_hbm_specs(n + m)),
        scratch_shapes=[pltpu.SemaphoreType.DMA((n, 3)), pltpu.SemaphoreType.DMA((n, 3)),
                        pltpu.SemaphoreType.DMA((m,)), pltpu.SemaphoreType.DMA((m,))],
        name="chip_and_sibling_exchange", compiler_params=pltpu.CompilerParams(has_side_effects=True))(*parts, *grads)
    return list(res[:n]), list(res[n:])


def reduce_layers(layer_grads, c_idx, chip_idx):
    depth = len(layer_grads)
    accs = [None] * len(layer_grads[0])
    recv = sibling_exchange_halves(layer_grads[0])
    for l in range(depth):
        parts = [pair_sum(g, r, c_idx) for g, r in zip(layer_grads[l], recv)]
        if l + 1 < depth:
            slots, recv = chip_and_sibling_exchange(parts, layer_grads[l + 1])
        else:
            slots = chip_exchange(parts)
        accs = [sum_slots_into(p, s, chip_idx, a, l, depth) for p, s, a in zip(parts, slots, accs)]
    return accs


def adamw_halves(w, mine, theirs, m, v, c_idx):
    L, R, C = w.shape
    half = R // 2
    tr = _pick(half, tuple(t for t in (2048, 1024, 512, 256, 128, 64, 32, 16, 8) if t * C * 4 <= (1 << 20)))
    nb = half // tr
    c1 = 1.0 / (1.0 - ADAM_B1 ** ADAM_STEP)
    c2 = 1.0 / (1.0 - ADAM_B2 ** ADAM_STEP)

    def body(c_ref, w_ref, a_ref, b_ref, m_ref, v_ref, g_ref, d_ref, nm_ref, nv_ref):
        gv = jnp.where(c_ref[0] == pl.program_id(1), a_ref[...], b_ref[...])
        nm = ADAM_B1 * m_ref[...] + (1.0 - ADAM_B1) * gv
        nv = ADAM_B2 * v_ref[...] + (1.0 - ADAM_B2) * (gv * gv)
        g_ref[...] = gv
        d_ref[...] = -ADAM_LR * ((nm * c1) / (jnp.sqrt(nv * c2) + ADAM_EPS) + ADAM_WD * w_ref[...])
        nm_ref[...] = nm
        nv_ref[...] = nv

    full = pl.BlockSpec((None, tr, C), lambda l, h, i, c_ref: (l, h * nb + i, 0))
    part = pl.BlockSpec((None, tr, C), lambda l, h, i, c_ref: (l, i, 0))
    grid_spec = pltpu.PrefetchScalarGridSpec(
        num_scalar_prefetch=1, grid=(L, 2, nb), in_specs=[full, part, part, full, full],
        out_specs=(full, full, full, full))
    out = jax.ShapeDtypeStruct((L, R, C), F32)
    return pl.pallas_call(body, out_shape=(out, out, out, out), grid_spec=grid_spec, name="adamw_halves",
                          compiler_params=_params(("parallel", "parallel", "parallel")))(c_idx, w, mine, theirs, m, v)


BIG = ("w_in", "mla_w_qb", "mla_w_kvb", "w_out", "xa_wq", "xa_wk", "xa_wv", "xa_wo", "ffn_w_up", "ffn_w_down")
COL_SHARDED = ("w_in", "mla_w_qb", "mla_w_kvb", "ffn_w_up")
SMALL_REPL = ("norm_mix", "dn_a_log", "dn_dt_bias", "dn_out_norm", "mla_q_norm", "mla_kv_norm", "mem_norm",
              "norm_xattn", "norm_ffn", "ffn_conv_bias", "norm_final")
SMALL_SHARDED = ("dn_conv", "ffn_conv")
WEIGHTS = ("norm_mix", "w_in", "dn_conv", "dn_a_log", "dn_dt_bias", "dn_out_norm", "mla_q_norm", "mla_w_qb",
           "mla_kv_norm", "mla_w_kvb", "w_out", "mem_norm", "norm_xattn", "xa_wq", "xa_wk", "xa_wv", "xa_wo",
           "norm_ffn", "ffn_w_up", "ffn_conv", "ffn_conv_bias", "ffn_w_down", "norm_final")


def _pad_cols(a, cp):
    c = a.shape[-1]
    if c == cp:
        return a
    return jnp.pad(a, [(0, 0)] * (a.ndim - 1) + [(0, cp - c)])


def _pack(arrs):
    flat = jnp.concatenate([a.reshape(-1).astype(F32) for a in arrs])
    rows = -(-flat.shape[0] // LANES)
    rows = -(-rows // 8) * 8
    return jnp.pad(flat, (0, rows * LANES - flat.shape[0])).reshape(rows, LANES)


def _unpack(buf, like):
    flat = buf.reshape(-1)
    out, off = [], 0
    for a in like:
        n = int(np.prod(a.shape))
        out.append(flat[off:off + n].reshape(a.shape))
        off += n
    return out


def _heads(t, h):
    s = t.shape[0]
    return jnp.transpose(t.reshape(s, h, t.shape[1] // h), (1, 0, 2))


def _unheads(t):
    h, s, d = t.shape
    return jnp.transpose(t, (1, 0, 2)).reshape(s, h * d)


def _rope(t, cos, sin):
    t1, t2 = jnp.split(t, 2, axis=-1)
    return jnp.concatenate([t1 * cos - t2 * sin, t2 * cos + t1 * sin], axis=-1)


def _forward_loss(p, x, mem, cos, sin, target, dims):
    S, D = x.shape
    depth, dn_h, mla_h, d_ff, c_in, c_in_pad, ffn_tn = dims
    dn_w = dn_h * DN_HEAD_DIM
    n_chunks = S // CHUNK
    mem_n = rms_norm(mem, p["mem_norm"], EPS)
    h = x
    ffn_blocks = (2 * d_ff) // ffn_tn
    half_blocks = ffn_blocks // 2

    def ffn_perm(nb):
        return jnp.where(nb < half_blocks, 2 * nb, 2 * (nb - half_blocks) + 1)

    def ffn_layout(a):
        k = a.shape[0]
        return jnp.transpose(a.reshape(k, 2, half_blocks, ffn_tn), (0, 2, 1, 3)).reshape(k, 2 * d_ff)

    for l in range(depth):
        u = rms_norm(h, p["norm_mix"][l], EPS)
        projp = linear(u, p["w_in"][l])
        proj = jnp.concatenate([projp[:, g * c_in_pad:g * c_in_pad + c_in] for g in range(N_CHIPS)], axis=1)
        o0 = 3 * dn_w
        dz = proj[:, o0:o0 + dn_w]
        db = proj[:, o0 + dn_w:o0 + dn_w + dn_h]
        da = proj[:, o0 + dn_w + dn_h:o0 + dn_w + 2 * dn_h]
        o1 = o0 + dn_w + 2 * dn_h
        mq = proj[:, o1:o1 + MLA_Q_RANK]
        mkv = proj[:, o1 + MLA_Q_RANK:]
        qkv = conv_silu(proj[:, :o0], p["dn_conv"][l])
        qd = group_l2_norm(qkv[:, :dn_w], dn_h)
        kd = group_l2_norm(qkv[:, dn_w:2 * dn_w], dn_h)
        vd = qkv[:, 2 * dn_w:]
        beta = jax.nn.sigmoid(db)
        g = -jnp.exp(p["dn_a_log"][l]) * jax.nn.softplus(da + p["dn_dt_bias"][l])
        g_t, beta_t = g.T, beta.T
        o_dn = delta_rule(qd, kd, vd, g_t[:, :, None], g_t.reshape(dn_h, n_chunks, 1, CHUNK), beta_t[:, :, None])
        o_dn = group_rms_norm(o_dn, p["dn_out_norm"][l], dn_h, EPS)
        o_dn = o_dn * jax.nn.silu(dz)
        qf = linear(rms_norm(mq, p["mla_q_norm"][l], EPS), p["mla_w_qb"][l]).reshape(S, mla_h, MLA_NOPE + MLA_ROPE)
        q_pe = _rope(qf[..., MLA_NOPE:], cos[:, None, :], sin[:, None, :])
        k_pe = _rope(mkv[:, MLA_KV_RANK:], cos, sin)
        kv = linear(rms_norm(mkv[:, :MLA_KV_RANK], p["mla_kv_norm"][l], EPS), p["mla_w_kvb"][l])
        kv = kv.reshape(S, mla_h, MLA_NOPE + MLA_V)
        qa = jnp.transpose(jnp.concatenate([qf[..., :MLA_NOPE], q_pe], axis=-1), (1, 0, 2))
        ka = jnp.transpose(jnp.concatenate(
            [kv[..., :MLA_NOPE], jnp.broadcast_to(k_pe[:, None, :], (S, mla_h, MLA_ROPE))], axis=-1), (1, 0, 2))
        va = jnp.transpose(kv[..., MLA_NOPE:], (1, 0, 2))
        o_mla = _unheads(attention(qa, ka, va, (MLA_NOPE + MLA_ROPE) ** -0.5, True, None))
        h = linear(jnp.concatenate([o_dn, o_mla], axis=-1), p["w_out"][l], res=h)
        hn = rms_norm(h, p["norm_xattn"][l], EPS)
        xo = attention(linear(hn, p["xa_wq"][l]), linear(mem_n, p["xa_wk"][l]), linear(mem_n, p["xa_wv"][l]),
                       (D // XA_HEADS) ** -0.5, False, XA_HEADS)
        h = linear(xo, p["xa_wo"][l], res=h)
        hn = rms_norm(h, p["norm_ffn"][l], EPS)
        pre = linear(hn, p["ffn_w_up"][l], tn=ffn_tn, colperm=ffn_perm)
        act = conv_glu(pre, ffn_layout(p["ffn_conv"][l]), ffn_layout(p["ffn_conv_bias"][l][None, :]), ffn_tn)
        h = linear(act, p["ffn_w_down"][l], res=h)
    y = rms_norm(h, p["norm_final"], EPS)
    return sq_loss(y, target)


def kernel(x, mem, positions, norm_mix, w_in, dn_conv, dn_a_log, dn_dt_bias, dn_out_norm, mla_q_norm, mla_w_qb, mla_kv_norm, mla_w_kvb, w_out, mem_norm, norm_xattn, xa_wq, xa_wk, xa_wv, xa_wo, norm_ffn, ffn_w_up, ffn_conv, ffn_conv_bias, ffn_w_down, norm_final, loss_target, m_norm_mix, m_w_in, m_dn_conv, m_dn_a_log, m_dn_dt_bias, m_dn_out_norm, m_mla_q_norm, m_mla_w_qb, m_mla_kv_norm, m_mla_w_kvb, m_w_out, m_mem_norm, m_norm_xattn, m_xa_wq, m_xa_wk, m_xa_wv, m_xa_wo, m_norm_ffn, m_ffn_w_up, m_ffn_conv, m_ffn_conv_bias, m_ffn_w_down, m_norm_final, v_norm_mix, v_w_in, v_dn_conv, v_dn_a_log, v_dn_dt_bias, v_dn_out_norm, v_mla_q_norm, v_mla_w_qb, v_mla_kv_norm, v_mla_w_kvb, v_w_out, v_mem_norm, v_norm_xattn, v_xa_wq, v_xa_wk, v_xa_wv, v_xa_wo, v_norm_ffn, v_ffn_w_up, v_ffn_conv, v_ffn_conv_bias, v_ffn_w_down, v_norm_final):
    w = dict(norm_mix=norm_mix, w_in=w_in, dn_conv=dn_conv, dn_a_log=dn_a_log, dn_dt_bias=dn_dt_bias, dn_out_norm=dn_out_norm, mla_q_norm=mla_q_norm, mla_w_qb=mla_w_qb, mla_kv_norm=mla_kv_norm, mla_w_kvb=mla_w_kvb, w_out=w_out, mem_norm=mem_norm, norm_xattn=norm_xattn, xa_wq=xa_wq, xa_wk=xa_wk, xa_wv=xa_wv, xa_wo=xa_wo, norm_ffn=norm_ffn, ffn_w_up=ffn_w_up, ffn_conv=ffn_conv, ffn_conv_bias=ffn_conv_bias, ffn_w_down=ffn_w_down, norm_final=norm_final)
    m = dict(norm_mix=m_norm_mix, w_in=m_w_in, dn_conv=m_dn_conv, dn_a_log=m_dn_a_log, dn_dt_bias=m_dn_dt_bias, dn_out_norm=m_dn_out_norm, mla_q_norm=m_mla_q_norm, mla_w_qb=m_mla_w_qb, mla_kv_norm=m_mla_kv_norm, mla_w_kvb=m_mla_w_kvb, w_out=m_w_out, mem_norm=m_mem_norm, norm_xattn=m_norm_xattn, xa_wq=m_xa_wq, xa_wk=m_xa_wk, xa_wv=m_xa_wv, xa_wo=m_xa_wo, norm_ffn=m_norm_ffn, ffn_w_up=m_ffn_w_up, ffn_conv=m_ffn_conv, ffn_conv_bias=m_ffn_conv_bias, ffn_w_down=m_ffn_w_down, norm_final=m_norm_final)
    v = dict(norm_mix=v_norm_mix, w_in=v_w_in, dn_conv=v_dn_conv, dn_a_log=v_dn_a_log, dn_dt_bias=v_dn_dt_bias, dn_out_norm=v_dn_out_norm, mla_q_norm=v_mla_q_norm, mla_w_qb=v_mla_w_qb, mla_kv_norm=v_mla_kv_norm, mla_w_kvb=v_mla_w_kvb, w_out=v_w_out, mem_norm=v_mem_norm, norm_xattn=v_norm_xattn, xa_wq=v_xa_wq, xa_wk=v_xa_wk, xa_wv=v_xa_wv, xa_wo=v_xa_wo, norm_ffn=v_norm_ffn, ffn_w_up=v_ffn_w_up, ffn_conv=v_ffn_conv, ffn_conv_bias=v_ffn_conv_bias, ffn_w_down=v_ffn_w_down, norm_final=v_norm_final)

    S, D = x.shape[1], x.shape[2]
    depth = w_in.shape[0]
    dn_h = (D // 2) // DN_HEAD_DIM
    mla_h = (D - dn_h * DN_HEAD_DIM) // MLA_V
    d_ff = ffn_w_down.shape[1] * N_CHIPS
    c_in = w_in.shape[2]
    c_in_pad = -(-c_in // LANES) * LANES
    ffn_tn = _pick(ffn_w_up.shape[2], (256, 128))
    dims = (depth, dn_h, mla_h, d_ff, c_in, c_in_pad, ffn_tn)
    chip = 2 * lax.axis_index("x") + lax.axis_index("y")
    c_idx = lax.axis_index("c").astype(jnp.int32).reshape(1)

    gathered = {n: [] for n in BIG}
    for l in range(depth):
        shards = []
        for n in BIG:
            s = w[n][l].astype(BF16)
            if n == "w_in":
                s = _pad_cols(s, c_in_pad)
            shards.append(s)
        for n, own, full in zip(BIG, shards, gather_weights(shards)):
            full = lax.dynamic_update_slice(full, own[None], (chip, 0, 0))
            if n not in COL_SHARDED:
                full = full.reshape(1, N_CHIPS * full.shape[1], full.shape[2])
            gathered[n].append(full)
    small_sh = small_allgather(_pack([w[n] for n in SMALL_SHARDED]))
    params = dict(gathered)
    for n, parts in zip(SMALL_SHARDED, zip(*[_unpack(small_sh[2 * j], [w[k] for k in SMALL_SHARDED])
                                              for j in range(N_CHIPS)])):
        params[n] = jnp.concatenate(parts, axis=-1)
    for n in SMALL_REPL:
        params[n] = w[n]

    inv = ROPE_BASE ** (-jnp.arange(0, MLA_ROPE, 2, dtype=F32) / MLA_ROPE)
    ang = positions[0].astype(F32)[:, None] * inv
    cos, sin = jnp.cos(ang), jnp.sin(ang)
    loss_fn = lambda p, xx: _forward_loss(p, xx, mem[0], cos, sin, loss_target[0], dims)
    loss_local, (gp, gx) = jax.value_and_grad(loss_fn, argnums=(0, 1))(params, x[0])

    small_names = SMALL_REPL + SMALL_SHARDED
    packed = _pack([loss_local.reshape(1)] + [gp[n] for n in small_names])
    summed = sum_blocks(small_allgather(packed))
    unpacked = _unpack(summed, [loss_local.reshape(1)] + [gp[n] for n in small_names])
    loss = unpacked[0][0]
    grads = dict(zip(small_names, unpacked[1:]))
    for n in SMALL_SHARDED:
        cs = w[n].shape[-1]
        grads[n] = lax.dynamic_slice_in_dim(grads[n], chip * cs, cs, axis=-1)

    chip_idx = chip.astype(jnp.int32).reshape(1)
    layer_grads = []
    for l in range(depth):
        gl = []
        for n in BIG:
            g = gp[n][l]
            if n not in COL_SHARDED:
                g = g.reshape(N_CHIPS, g.shape[1] // N_CHIPS, g.shape[2])
            gl.append(g)
        layer_grads.append(gl)
    accs = reduce_layers(layer_grads, c_idx, chip_idx)
    theirs = sibling_swap(accs)

    delta, new_m, new_v = {}, {}, {}
    for n, mine, other in zip(BIG, accs, theirs):
        if n == "w_in":
            lo, hi = jnp.where(c_idx[0] == 0, mine, other), jnp.where(c_idx[0] == 0, other, mine)
            grads[n] = jnp.concatenate([lo, hi], axis=1)[:, :, :c_in]
            delta[n], new_m[n], new_v[n] = adamw(w[n], grads[n], m[n], v[n])
        else:
            grads[n], delta[n], new_m[n], new_v[n] = adamw_halves(w[n], mine, other, m[n], v[n], c_idx)
    sw, sg, sm, sv = (_pack([d[n] for n in small_names]) for d in (w, grads, m, v))
    sd, snm, snv = adamw(sw, sg, sm, sv)
    like = [w[n] for n in small_names]
    for d, buf in ((delta, sd), (new_m, snm), (new_v, snv)):
        for n, a in zip(small_names, _unpack(buf, like)):
            d[n] = a

    return (loss, gx[None], *[grads[n] for n in WEIGHTS], *[delta[n] for n in WEIGHTS],
            *[new_m[n] for n in WEIGHTS], *[new_v[n] for n in WEIGHTS])
```
